```python
import math
import jax
import jax.numpy as jnp
from jax import lax
import numpy as np

D_MODEL = 2048
BATCH = 8
SEQ = 2048
DEPTH = 2

GRID_W = 64
CTX_LEN = 256

D_SSM = D_MODEL
SSM_HEAD_DIM = 64
SSM_HEADS = D_SSM // SSM_HEAD_DIM
SSM_GROUPS = 4
SSM_STATE = 128
CONV_WIDTH = 5
CONV_DIM = D_SSM + 2 * SSM_GROUPS * SSM_STATE
SSD_CHUNK = 128

D_ATT = D_MODEL
ATT_HEAD_DIM = 64
ATT_HEADS = D_ATT // (2 * ATT_HEAD_DIM)
Q_BLOCK = 128
ROPE_BASE = 10000.0

IN_COLS = D_SSM + CONV_DIM + 2 * SSM_HEADS + 3 * D_ATT + 2 * D_MODEL

MOE_GROUPS = 4
EXPERTS_PER_GROUP = 8
N_EXPERTS = MOE_GROUPS * EXPERTS_PER_GROUP
MOE_TOP_K = 2
EXPERT_FF = D_MODEL // 2
MOE_BLOCK = 256

ALPHA = (2.0 * DEPTH) ** 0.25
BETA = (8.0 * DEPTH) ** -0.25
LN_EPS = 1e-6
RMS_EPS = 1e-5

kernel_name = "hybrid_ssd_diffattn_hmoe_dit"

F32 = jnp.float32


def layer_norm(t, g=None, b=None):
    tf = t.astype(F32)
    mu = jnp.mean(tf, axis=-1, keepdims=True)
    var = jnp.mean(jnp.square(tf - mu), axis=-1, keepdims=True)
    y = (tf - mu) * lax.rsqrt(var + LN_EPS)
    if g is not None:
        y = y * g.astype(F32) + b.astype(F32)
    return y.astype(t.dtype)


def modulate(t, shift, scale):
    return t * (1.0 + scale) + shift


def grid_positions(seq_len):
    rows = seq_len // GRID_W
    row = jnp.repeat(jnp.arange(rows, dtype=F32), GRID_W)
    col = jnp.tile(jnp.arange(GRID_W, dtype=F32), rows)
    return row, col


def split_proj(p):
    sizes = (D_SSM, CONV_DIM, 2 * SSM_HEADS, D_ATT, D_ATT, D_ATT, 2 * D_MODEL)
    idx = [int(i) for i in np.cumsum(sizes)[:-1]]
    return jnp.split(p, idx, axis=-1)


def dwconv_centred(t, w, b):
    pad = (CONV_WIDTH - 1) // 2
    out = lax.conv_general_dilated(
        t, w[:, None, :].astype(t.dtype), window_strides=(1,),
        padding=((pad, pad),), dimension_numbers=('NWC', 'WIO', 'NWC'),
        feature_group_count=t.shape[-1])
    return out + b.astype(t.dtype)


def ssd_inputs(xbc, dt_raw, conv_w, conv_b, dt_bias, a_log):
    xbc = jax.nn.silu(dwconv_centred(xbc, conv_w, conv_b))
    xs, bm, cm = jnp.split(xbc, [D_SSM, D_SSM + SSM_GROUPS * SSM_STATE], axis=-1)
    b_, s_ = xs.shape[:2]
    xs = xs.reshape(b_, s_, SSM_HEADS, SSM_HEAD_DIM)
    bm = bm.reshape(b_, s_, SSM_GROUPS, SSM_STATE)
    cm = cm.reshape(b_, s_, SSM_GROUPS, SSM_STATE)
    dt = jax.nn.softplus((dt_raw.reshape(b_, s_, 2, SSM_HEADS) + dt_bias).astype(F32))
    a = -jnp.exp(a_log.astype(F32))
    return xs, bm, cm, dt, a


def segsum(a):
    t = a.shape[-1]
    cs = jnp.cumsum(a, axis=-1)
    diff = cs[..., :, None] - cs[..., None, :]
    return jnp.where(jnp.tril(jnp.ones((t, t), bool)), diff, -jnp.inf)


def ssd_chunked(xdt, adt, bm, cm, h0, with_y):
    b_, s_ = xdt.shape[:2]
    nc = s_ // SSD_CHUNK
    hg = SSM_HEADS // SSM_GROUPS
    xc = xdt.astype(F32).reshape(b_, nc, SSD_CHUNK, SSM_GROUPS, hg, SSM_HEAD_DIM)
    bc = bm.astype(F32).reshape(b_, nc, SSD_CHUNK, SSM_GROUPS, SSM_STATE)
    cc = cm.astype(F32).reshape(b_, nc, SSD_CHUNK, SSM_GROUPS, SSM_STATE)
    a = adt.astype(F32).reshape(b_, nc, SSD_CHUNK, SSM_GROUPS, hg).transpose(0, 3, 4, 1, 2)
    a_cs = jnp.cumsum(a, axis=-1)
    decay_to_end = jnp.exp(a_cs[..., -1:] - a_cs)
    chunk_states = jnp.einsum('bclgn,bghcl,bclghp->bcghpn', bc, decay_to_end, xc)
    h0 = h0.astype(F32).reshape(b_, SSM_GROUPS, hg, SSM_HEAD_DIM, SSM_STATE)
    chunk_states = jnp.concatenate([h0[:, None], chunk_states], axis=1)
    a_tot = jnp.pad(a_cs[..., -1], ((0, 0), (0, 0), (0, 0), (1, 0)))
    decay_chunk = jnp.exp(segsum(a_tot))
    states = jnp.einsum('bghzc,bcghpn->bzghpn', decay_chunk, chunk_states)
    final = states[:, -1].reshape(b_, SSM_HEADS, SSM_HEAD_DIM, SSM_STATE)
    if not with_y:
        return None, final
    seg = jnp.exp(segsum(a))
    cb = jnp.einsum('bclgn,bcsgn->bcgls', cc, bc)
    y_diag = jnp.einsum('bcgls,bghcls,bcsghp->bclghp', cb, seg, xc)
    y_off = jnp.einsum('bclgn,bcghpn,bghcl->bclghp', cc, states[:, :-1], jnp.exp(a_cs))
    return (y_diag + y_off).reshape(b_, s_, SSM_HEADS, SSM_HEAD_DIM), final


def maybe_flip(t, rev):
    return jnp.flip(t, axis=1) if rev else t


def bidir_ssd(lat, cin, d_skip, with_ctx):
    xs, bm, cm, dt, a = lat
    cxs, cbm, ccm, cdt, _ = cin
    b_ = xs.shape[0]
    dsk = d_skip.astype(F32)[:, None]
    y = dsk * xs.astype(F32)
    y_ctx = dsk * cxs.astype(F32) if with_ctx else None
    for direction in range(2):
        rev = direction == 1
        h0 = jnp.zeros((b_, SSM_HEADS, SSM_HEAD_DIM, SSM_STATE), F32)
        yc, hc = ssd_chunked(maybe_flip(cxs * cdt[:, :, direction, :, None], rev),
                             maybe_flip(cdt[:, :, direction] * a[direction], rev),
                             maybe_flip(cbm, rev), maybe_flip(ccm, rev), h0, with_ctx)
        yl, _ = ssd_chunked(maybe_flip(xs * dt[:, :, direction, :, None], rev),
                            maybe_flip(dt[:, :, direction] * a[direction], rev),
                            maybe_flip(bm, rev), maybe_flip(cm, rev), hc, True)
        y = y + maybe_flip(yl, rev)
        if with_ctx:
            y_ctx = y_ctx + maybe_flip(yc, rev)
    return y, y_ctx


def gated_rmsnorm(y, z, g):
    b_, s_ = y.shape[:2]
    yz = y.reshape(b_, s_, SSM_GROUPS, -1) * jax.nn.silu(z.astype(F32)).reshape(b_, s_, SSM_GROUPS, -1)
    yz = yz * lax.rsqrt(jnp.mean(jnp.square(yz), axis=-1, keepdims=True) + RMS_EPS)
    return (yz.reshape(b_, s_, D_SSM) * g.astype(F32)).astype(z.dtype)


def axial_rope(t, row, col):
    d = t.shape[-1]
    half = d // 2
    nf = half // 2
    inv = ROPE_BASE ** (-(jnp.arange(nf, dtype=F32) / nf))

    def rot(u, pos):
        ang = pos[:, None] * inv[None, :]
        cos = jnp.cos(ang)[:, None, None, :]
        sin = jnp.sin(ang)[:, None, None, :]
        u1, u2 = u[..., :nf], u[..., nf:]
        return jnp.concatenate([u1 * cos - u2 * sin, u1 * sin + u2 * cos], axis=-1)

    out = jnp.concatenate([rot(t[..., :half], row), rot(t[..., half:], col)], axis=-1)
    return out.astype(t.dtype)


def diff_combine(scores, v, lam):
    p = jax.nn.softmax(scores.astype(F32) * (ATT_HEAD_DIM ** -0.5), axis=-1)
    pd = p[:, :, 0] - lam * p[:, :, 1]
    return jnp.einsum('bhqk,bkhe->bqhe', pd.astype(v.dtype), v)


def head_rmsnorm(o, g):
    of = o.astype(F32)
    return of * lax.rsqrt(jnp.mean(jnp.square(of), axis=-1, keepdims=True) + RMS_EPS) * g.astype(F32)


def diff_attention(q, k, v, cq, ck, cv, lam, subln_g, lam_init, row, col, with_ctx):
    b_, s_ = q.shape[:2]
    s_ctx = cq.shape[1]

    def qk_heads(t):
        return t.reshape(t.shape[:2] + (ATT_HEADS, 2, ATT_HEAD_DIM))

    def v_heads(t):
        return t.reshape(t.shape[:2] + (ATT_HEADS, 2 * ATT_HEAD_DIM))

    q, k, cq, ck = qk_heads(q), qk_heads(k), qk_heads(cq), qk_heads(ck)
    v, cv = v_heads(v), v_heads(cv)
    q_rot, k_rot = axial_rope(q, row, col), axial_rope(k, row, col)
    v_all = jnp.concatenate([cv, v], axis=1)
    n_blk = s_ // Q_BLOCK

    def to_blocks(t):
        return jnp.moveaxis(t.reshape((b_, n_blk, Q_BLOCK) + t.shape[2:]), 1, 0)

    def query_block(args):
        qr, qn = args
        s_lat = jnp.einsum('bqhmd,bkhmd->bhmqk', qr, k_rot)
        s_cx = jnp.einsum('bqhmd,bkhmd->bhmqk', qn, ck)
        return diff_combine(jnp.concatenate([s_cx, s_lat], axis=-1), v_all, lam)

    o = lax.map(query_block, (to_blocks(q_rot), to_blocks(q)))
    o = jnp.moveaxis(o, 0, 1).reshape(b_, s_, ATT_HEADS, 2 * ATT_HEAD_DIM)
    y = (head_rmsnorm(o, subln_g) * (1.0 - lam_init)).reshape(b_, s_, D_ATT).astype(v.dtype)
    y_ctx = None
    if with_ctx:
        o_c = diff_combine(jnp.einsum('bqhmd,bkhmd->bhmqk', cq, ck), cv, lam)
        y_ctx = (head_rmsnorm(o_c, subln_g) * (1.0 - lam_init)).reshape(b_, s_ctx, D_ATT).astype(v.dtype)
    return y, y_ctx


def gated_merge(y_ssm, y_att, gates, w_br_ssm, w_br_att, w_out):
    g_ssm, g_att = jnp.split(gates, 2, axis=-1)
    m = jax.nn.sigmoid(g_ssm) * (y_ssm @ w_br_ssm) + jax.nn.sigmoid(g_att) * (y_att @ w_br_att)
    return m @ w_out


def token_mixer(u, u_ctx, row, col, w_in, conv_w, conv_b, dt_bias, a_log, d_skip, norm_g,
                lq1, lk1, lq2, lk2, subln_g, w_br_ssm, w_br_att, w_out, lam_init, with_ctx):
    z, xbc, dt_raw, q, k, v, gates = split_proj(u @ w_in)
    cz, cxbc, cdt, cq, ck, cv, cgates = split_proj(u_ctx @ w_in)
    lat = ssd_inputs(xbc, dt_raw, conv_w, conv_b, dt_bias, a_log)
    cin = ssd_inputs(cxbc, cdt, conv_w, conv_b, dt_bias, a_log)
    y_s, y_s_ctx = bidir_ssd(lat, cin, d_skip, with_ctx)
    y_s = gated_rmsnorm(y_s, z, norm_g)
    lam = (jnp.exp(jnp.sum(lq1.astype(F32) * lk1.astype(F32)))
           - jnp.exp(jnp.sum(lq2.astype(F32) * lk2.astype(F32))) + lam_init)
    y_a, y_a_ctx = diff_attention(q, k, v, cq, ck, cv, lam, subln_g, lam_init, row, col, with_ctx)
    o = gated_merge(y_s, y_a, gates, w_br_ssm, w_br_att, w_out)
    o_ctx = None
    if with_ctx:
        y_s_ctx = gated_rmsnorm(y_s_ctx, cz, norm_g)
        o_ctx = gated_merge(y_s_ctx, y_a_ctx, cgates, w_br_ssm, w_br_att, w_out)
    return o, o_ctx


def hier_moe(t, w_rg, b_rg, w_re, b_re, w_gate, w_up, w_down):
    n_tok = t.shape[0]
    pg = jax.nn.softmax((t @ w_rg + b_rg).astype(F32), axis=-1)
    p_grp, grp = lax.top_k(pg, 1)
    le = (t @ w_re + b_re).astype(F32).reshape(n_tok, MOE_GROUPS, EXPERTS_PER_GROUP)
    le = jnp.take_along_axis(le, grp[:, :, None], axis=1)[:, 0]
    p_exp, e_loc = lax.top_k(jax.nn.softmax(le, axis=-1), MOE_TOP_K)
    wts = p_grp * p_exp / jnp.sum(p_exp, axis=-1, keepdims=True)
    expert = grp * EXPERTS_PER_GROUP + e_loc
    n_assign = n_tok * MOE_TOP_K
    e_flat = expert.reshape(-1)
    tok_flat = jnp.repeat(jnp.arange(n_tok, dtype=jnp.int32), MOE_TOP_K)
    w_flat = wts.reshape(-1)
    order = jnp.argsort(e_flat)
    e_s, tok_s, w_s = e_flat[order], tok_flat[order], w_flat[order]
    counts = jnp.zeros((N_EXPERTS,), jnp.int32).at[e_flat].add(1)
    starts = jnp.cumsum(counts) - counts
    padded = (counts + MOE_BLOCK - 1) // MOE_BLOCK * MOE_BLOCK
    pstarts = jnp.cumsum(padded) - padded
    dest = pstarts[e_s] + jnp.arange(n_assign, dtype=jnp.int32) - starts[e_s]
    n_blocks = -(-n_assign // MOE_BLOCK) + N_EXPERTS
    slot_tok = jnp.zeros((n_blocks * MOE_BLOCK,), jnp.int32).at[dest].set(tok_s)
    slot_w = jnp.zeros((n_blocks * MOE_BLOCK,), F32).at[dest].set(w_s)
    block_start = jnp.arange(n_blocks, dtype=jnp.int32) * MOE_BLOCK
    block_exp = jnp.minimum(
        jnp.sum(block_start[:, None] >= (pstarts + padded)[None, :], axis=1), N_EXPERTS - 1)

    def expert_block(args):
        idx, e = args
        xb = t[idx]
        hdn = jax.nn.silu(xb @ w_gate[e]) * (xb @ w_up[e])
        return hdn @ w_down[e]

    y_slots = lax.map(expert_block, (slot_tok.reshape(n_blocks, MOE_BLOCK), block_exp))
    y_slots = y_slots.reshape(-1, t.shape[-1]) * slot_w[:, None].astype(t.dtype)
    return jnp.zeros_like(t).at[slot_tok].add(y_slots)


def setup_inputs(seed: int = 0) -> dict:
    key = jax.random.key(seed)
    ks = jax.random.split(key, 32)
    L = DEPTH

    def nrm(k, shape, scale):
        return jax.random.normal(k, shape, F32) * scale

    dt0 = jnp.exp(jax.random.uniform(ks[8], (L, 2, SSM_HEADS), F32, math.log(1e-3), math.log(1e-1)))
    return {
        "x": nrm(ks[0], (BATCH, SEQ, D_MODEL), 1.0),
        "c": nrm(ks[1], (BATCH, D_MODEL), 1.0),
        "ctx": nrm(ks[2], (BATCH, CTX_LEN, D_MODEL), 1.0),
        "c_ctx": nrm(ks[3], (D_MODEL,), 1.0),
        "w_ada": nrm(ks[4], (L, D_MODEL, 6 * D_MODEL), D_MODEL ** -0.5),
        "b_ada": nrm(ks[5], (L, 6 * D_MODEL), 0.01),
        "w_in": nrm(ks[6], (L, D_MODEL, IN_COLS), D_MODEL ** -0.5),
        "conv_w": nrm(ks[7], (L, CONV_WIDTH, CONV_DIM), CONV_WIDTH ** -0.5),
        "conv_b": nrm(ks[9], (L, CONV_DIM), 0.01),
        "ssm_dt_bias": dt0 + jnp.log(-jnp.expm1(-dt0)),
        "ssm_a_log": jnp.log(jax.random.uniform(ks[10], (L, 2, SSM_HEADS), F32, 1.0, 16.0)),
        "ssm_d": 1.0 + nrm(ks[11], (L, SSM_HEADS), 0.1),
        "ssm_norm_g": 1.0 + nrm(ks[12], (L, D_SSM), 0.1),
        "lam_q1": nrm(ks[13], (L, ATT_HEAD_DIM), 0.1),
        "lam_k1": nrm(ks[14], (L, ATT_HEAD_DIM), 0.1),
        "lam_q2": nrm(ks[15], (L, ATT_HEAD_DIM), 0.1),
        "lam_k2": nrm(ks[16], (L, ATT_HEAD_DIM), 0.1),
        "attn_subln_g": 1.0 + nrm(ks[17], (L, 2 * ATT_HEAD_DIM), 0.1),
        "w_br_ssm": nrm(ks[18], (L, D_SSM, D_MODEL), D_SSM ** -0.5),
        "w_br_att": nrm(ks[19], (L, D_ATT, D_MODEL), D_ATT ** -0.5),
        "w_out": nrm(ks[20], (L, D_MODEL, D_MODEL), BETA * D_MODEL ** -0.5),
        "ln1_g": 1.0 + nrm(ks[21], (L, D_MODEL), 0.1),
        "ln1_b": nrm(ks[22], (L, D_MODEL), 0.01),
        "w_router_group": nrm(ks[23], (L, D_MODEL, MOE_GROUPS), D_MODEL ** -0.5),
        "b_router_group": nrm(ks[24], (L, MOE_GROUPS), 0.01),
        "w_router_expert": nrm(ks[25], (L, D_MODEL, N_EXPERTS), D_MODEL ** -0.5),
        "b_router_expert": nrm(ks[26], (L, N_EXPERTS), 0.01),
        "w_exp_gate": nrm(ks[27], (L, N_EXPERTS, D_MODEL, EXPERT_FF), D_MODEL ** -0.5),
        "w_exp_up": nrm(ks[28], (L, N_EXPERTS, D_MODEL, EXPERT_FF), D_MODEL ** -0.5),
        "w_exp_down": nrm(ks[29], (L, N_EXPERTS, EXPERT_FF, D_MODEL), BETA * EXPERT_FF ** -0.5),
        "ln2_g": 1.0 + nrm(ks[30], (L, D_MODEL), 0.1),
        "ln2_b": nrm(ks[31], (L, D_MODEL), 0.01),
    }


def reference(x, c, ctx, c_ctx, w_ada, b_ada, w_in, conv_w, conv_b, ssm_dt_bias, ssm_a_log,
              ssm_d, ssm_norm_g, lam_q1, lam_k1, lam_q2, lam_k2, attn_subln_g, w_br_ssm,
              w_br_att, w_out, ln1_g, ln1_b, w_router_group, b_router_group, w_router_expert,
              b_router_expert, w_exp_gate, w_exp_up, w_exp_down, ln2_g, ln2_b):
    row, col = grid_positions(x.shape[1])
    h, h_ctx = x, ctx
    sc = jax.nn.silu(c)
    sc_ctx = jax.nn.silu(c_ctx)
    for l in range(DEPTH):
        last = l == DEPTH - 1
        lam_init = 0.8 - 0.6 * math.exp(-0.3 * l)
        mod = (sc @ w_ada[l] + b_ada[l])[:, None, :]
        mod_ctx = sc_ctx @ w_ada[l] + b_ada[l]
        sh1, s1, g1, sh2, s2, g2 = jnp.split(mod, 6, axis=-1)
        csh1, cs1, cg1, csh2, cs2, cg2 = jnp.split(mod_ctx, 6, axis=-1)

        u = modulate(layer_norm(h), sh1, s1)
        u_ctx = modulate(layer_norm(h_ctx), csh1, cs1)
        o, o_ctx = token_mixer(u, u_ctx, row, col, w_in[l], conv_w[l], conv_b[l], ssm_dt_bias[l],
                               ssm_a_log[l], ssm_d[l], ssm_norm_g[l], lam_q1[l], lam_k1[l],
                               lam_q2[l], lam_k2[l], attn_subln_g[l], w_br_ssm[l], w_br_att[l],
                               w_out[l], lam_init, not last)
        h = layer_norm(ALPHA * h + g1 * o, ln1_g[l], ln1_b[l])

        u = modulate(layer_norm(h), sh2, s2)
        moe_w = (w_router_group[l], b_router_group[l], w_router_expert[l], b_router_expert[l],
                 w_exp_gate[l], w_exp_up[l], w_exp_down[l])
        if last:
            y = hier_moe(u.reshape(-1, D_MODEL), *moe_w).reshape(h.shape)
        else:
            h_ctx = layer_norm(ALPHA * h_ctx + cg1 * o_ctx, ln1_g[l], ln1_b[l])
            u_ctx = modulate(layer_norm(h_ctx), csh2, cs2)
            n_ctx = u_ctx.shape[0] * u_ctx.shape[1]
            y_all = hier_moe(jnp.concatenate([u_ctx.reshape(-1, D_MODEL), u.reshape(-1, D_MODEL)], axis=0), *moe_w)
            y_ctx = y_all[:n_ctx].reshape(h_ctx.shape)
            y = y_all[n_ctx:].reshape(h.shape)
            h_ctx = layer_norm(ALPHA * h_ctx + cg2 * y_ctx, ln2_g[l], ln2_b[l])
        h = layer_norm(ALPHA * h + g2 * y, ln2_g[l], ln2_b[l])
    return h
```

```python
import functools
import math

import jax
import jax.numpy as jnp
from jax import lax
from jax.experimental import pallas as pl
from jax.experimental.pallas import tpu as pltpu

F32 = jnp.float32
BF16 = jnp.bfloat16
HIGHEST = lax.Precision.HIGHEST

D_MODEL = 2048
GRID_W = 64

SSM_HEAD_DIM = 64
SSM_HEADS = D_MODEL // SSM_HEAD_DIM
SSM_GROUPS = 4
SSM_STATE = 128
HEADS_PER_GROUP = SSM_HEADS // SSM_GROUPS
GROUP_WIDTH = HEADS_PER_GROUP * SSM_HEAD_DIM
CONV_WIDTH = 5
CONV_DIM = D_MODEL + 2 * SSM_GROUPS * SSM_STATE
SSD_CHUNK = 128

ATT_HEAD_DIM = 64
ATT_HEADS = D_MODEL // (2 * ATT_HEAD_DIM)
ROPE_BASE = 10000.0

MOE_GROUPS = 4
EXPERTS_PER_GROUP = 8
N_EXPERTS = MOE_GROUPS * EXPERTS_PER_GROUP
MOE_TOP_K = 2
EXPERT_FF = D_MODEL // 2

LN_EPS = 1e-6
RMS_EPS = 1e-5

LANES = 128
MOD_ROWS = 16
TILE = 256
MOE_BLOCK = 256
VMEM_CAP = 60 * 1024 * 1024
VMEM_SLACK = 6 * 1024 * 1024


def _params(sem, block_bytes):
    return pltpu.CompilerParams(dimension_semantics=sem,
                                vmem_limit_bytes=int(min(VMEM_CAP, 2 * block_bytes + VMEM_SLACK)))


def _ln(x):
    mu = jnp.mean(x, axis=-1, keepdims=True)
    xc = x - mu
    var = jnp.mean(xc * xc, axis=-1, keepdims=True)
    return xc * lax.rsqrt(var + LN_EPS)


def _silu(x):
    return x * jax.nn.sigmoid(x)


def _mod_kernel(c_ref, w_ref, b_ref, o_ref):
    sc = _silu(c_ref[...]).astype(BF16)
    o_ref[...] = jnp.dot(sc, w_ref[...].astype(BF16), preferred_element_type=F32) + b_ref[...]


def _adaln(c_all, w_ada, b_ada):
    depth, d, n = w_ada.shape
    tn = 1024
    return pl.pallas_call(
        _mod_kernel,
        out_shape=jax.ShapeDtypeStruct((depth, MOD_ROWS, n), F32),
        grid=(depth, n // tn),
        in_specs=[pl.BlockSpec((MOD_ROWS, d), lambda l, j: (0, 0)),
                  pl.BlockSpec((None, d, tn), lambda l, j: (l, 0, j)),
                  pl.BlockSpec((None, 1, tn), lambda l, j: (l, 0, j))],
        out_specs=pl.BlockSpec((None, MOD_ROWS, tn), lambda l, j: (l, 0, j)),
        compiler_params=_params(("parallel", "parallel"), d * tn * 4 + d * tn * 2),
        name="adaln_mod",
    )(c_all, w_ada, b_ada.reshape(depth, 1, n))


class _Layout:
    def __init__(self, batch, seq, ctx):
        assert ctx % TILE == 0 and seq % TILE == 0 and seq % GRID_W == 0
        self.batch, self.seq, self.ctx = batch, seq, ctx
        self.nt = ctx + seq
        self.m = batch * self.nt
        self.tiles_per_batch = self.nt // TILE
        self.ctx_tiles = ctx // TILE
        self.n_tiles = self.m // TILE

    def mod_row(self, i):
        return jnp.where(i % self.tiles_per_batch < self.ctx_tiles, self.batch, i // self.tiles_per_batch)

    def mod_spec(self, chunk):
        return pl.BlockSpec((None, 1, D_MODEL), lambda i, *_: (self.mod_row(i), 0, chunk))


def _lnmod_kernel(h_ref, sh_ref, sc_ref, u_ref):
    u_ref[...] = (_ln(h_ref[...]) * (1.0 + sc_ref[...]) + sh_ref[...]).astype(u_ref.dtype)


def _lnmod(lay, h, mod3, shift_chunk, scale_chunk, out_dtype):
    row = pl.BlockSpec((TILE, D_MODEL), lambda i: (i, 0))
    return pl.pallas_call(
        _lnmod_kernel,
        out_shape=jax.ShapeDtypeStruct((lay.m, D_MODEL), out_dtype),
        grid=(lay.n_tiles,),
        in_specs=[row, lay.mod_spec(shift_chunk), lay.mod_spec(scale_chunk)],
        out_specs=row,
        compiler_params=_params(("parallel",), 2 * TILE * D_MODEL * 4),
        name="ln_modulate",
    )(h, mod3, mod3)


def _mm_kernel(a_ref, w_ref, o_ref):
    o_ref[...] = jnp.dot(a_ref[...], w_ref[...], preferred_element_type=F32).astype(o_ref.dtype)


def _matmul(a, w, out_dtype=F32, tm=1024, tn=1024):
    m, k = a.shape
    n = w.shape[1]
    tn = min(tn, n)
    tm = math.gcd(m, tm)
    assert tm % TILE == 0 and n % tn == 0
    return pl.pallas_call(
        _mm_kernel,
        out_shape=jax.ShapeDtypeStruct((m, n), out_dtype),
        grid=(m // tm, n // tn),
        in_specs=[pl.BlockSpec((tm, k), lambda i, j: (i, 0)),
                  pl.BlockSpec((k, tn), lambda i, j: (0, j))],
        out_specs=pl.BlockSpec((tm, tn), lambda i, j: (i, j)),
        compiler_params=_params(("parallel", "parallel"), tm * k * 2 + k * tn * 2 + tm * tn * 4),
        name="matmul",
    )(a, w)


def _conv_kernel(x_ref, w_ref, b_ref, o_ref, *, ctx):
    x = x_ref[...]
    nt = x.shape[0]
    t = lax.broadcasted_iota(jnp.int32, x.shape, 0)
    lo = jnp.where(t < ctx, 0, ctx)
    hi = jnp.where(t < ctx, ctx, nt)
    pad = (CONV_WIDTH - 1) // 2
    acc = b_ref[...] + x * w_ref[pad:pad + 1, :]
    for k in range(CONV_WIDTH):
        d = k - pad
        if d == 0:
            continue
        shifted = pltpu.roll(x, (-d) % nt, axis=0)
        valid = (t + d >= lo) & (t + d < hi)
        acc = acc + jnp.where(valid, shifted, 0.0) * w_ref[k:k + 1, :]
    o_ref[...] = _silu(acc)


def _conv_silu(lay, zxbc, conv_w, conv_b):
    tc = 512
    off = D_MODEL // tc
    return pl.pallas_call(
        functools.partial(_conv_kernel, ctx=lay.ctx),
        out_shape=jax.ShapeDtypeStruct((lay.m, CONV_DIM), F32),
        grid=(lay.batch, CONV_DIM // tc),
        in_specs=[pl.BlockSpec((lay.nt, tc), lambda b, j: (b, off + j)),
                  pl.BlockSpec((CONV_WIDTH, tc), lambda b, j: (0, j)),
                  pl.BlockSpec((1, tc), lambda b, j: (0, j))],
        out_specs=pl.BlockSpec((lay.nt, tc), lambda b, j: (b, j)),
        compiler_params=_params(("parallel", "parallel"), 6 * lay.nt * tc * 4),
        name="conv_silu",
    )(zxbc, conv_w, conv_b.reshape(1, CONV_DIM))


def _ssd_direction(x_ref, b_ref, c_ref, dt_ref, bias_ref, alog_ref, st_ref, y_ref, *, lane_off, upper):
    L = SSD_CHUNK
    r = lax.broadcasted_iota(jnp.int32, (L, L), 0)
    s = lax.broadcasted_iota(jnp.int32, (L, L), 1)
    tri = (s >= r) if upper else (s <= r)
    x = x_ref[...]
    bm = b_ref[...]
    cm = c_ref[...].astype(BF16)
    dtv = dt_ref[...] + bias_ref[...]
    dt = jnp.maximum(dtv, 0.0) + jnp.log1p(jnp.exp(-jnp.abs(dtv)))
    adt = dt * (-jnp.exp(alog_ref[...]))
    cs = jnp.dot(tri.astype(F32), adt, precision=HIGHEST, preferred_element_type=F32)
    cs_t = cs.T
    a_tot = jnp.sum(adt, axis=0, keepdims=True)
    cb = lax.dot_general(cm, bm.astype(BF16), (((1,), (1,)), ((), ())), preferred_element_type=F32)
    y_off = jnp.dot(cm, st_ref[...].astype(BF16), preferred_element_type=F32)
    bm_t = bm.T.astype(BF16)
    first_half = lax.broadcasted_iota(jnp.int32, (L, LANES), 1) < SSM_HEAD_DIM
    first_half_row = first_half[0:1, :]
    for pair in range(HEADS_PER_GROUP // 2):
        la = lane_off + 2 * pair
        lb = la + 1
        cols = slice(pair * LANES, (pair + 1) * LANES)
        cs_pair = jnp.where(first_half, cs[:, la:la + 1], cs[:, lb:lb + 1])
        dt_pair = jnp.where(first_half, dt[:, la:la + 1], dt[:, lb:lb + 1])
        tot_pair = jnp.where(first_half_row, a_tot[:, la:la + 1], a_tot[:, lb:lb + 1])
        xdt = x[:, cols] * dt_pair
        y_pair = y_off[:, cols] * jnp.exp(cs_pair)
        for lane, keep in ((la, first_half), (lb, jnp.logical_not(first_half))):
            seg = cs[:, lane:lane + 1] - cs_t[lane:lane + 1, :]
            decay = jnp.where(tri, jnp.exp(jnp.where(tri, seg, 0.0)), 0.0)
            mh = (cb * decay).astype(BF16)
            y_pair = y_pair + jnp.dot(mh, jnp.where(keep, xdt, 0.0).astype(BF16), preferred_element_type=F32)
        y_ref[:, cols] = y_pair
        z = (xdt * jnp.exp(tot_pair - cs_pair)).astype(BF16)
        st_ref[:, cols] = st_ref[:, cols] * jnp.exp(tot_pair) + jnp.dot(bm_t, z, preferred_element_type=F32)


def _ssd_kernel(xf, bf, cf, dtf, xr, br, cr, dtr, bias_ref, alog_ref, yf_ref, yr_ref, stf_ref, str_ref):
    @pl.when(pl.program_id(2) == 0)
    def _():
        stf_ref[...] = jnp.zeros_like(stf_ref)
        str_ref[...] = jnp.zeros_like(str_ref)

    _ssd_direction(xf, bf, cf, dtf, bias_ref, alog_ref, stf_ref, yf_ref, lane_off=0, upper=False)
    _ssd_direction(xr, br, cr, dtr, bias_ref, alog_ref, str_ref, yr_ref, lane_off=HEADS_PER_GROUP, upper=True)


def _ssd(lay, xbc, dt_raw, dt_bias_g, a_log_g):
    L = SSD_CHUNK
    nc = lay.nt // L
    ctx_chunks = lay.ctx // L
    b_col = D_MODEL // SSM_STATE
    c_col = b_col + SSM_GROUPS

    def fwd(t):
        return t

    def rev(t):
        return jnp.where(t < ctx_chunks, ctx_chunks - 1 - t, nc - 1 - (t - ctx_chunks))

    def specs(order):
        return [pl.BlockSpec((L, GROUP_WIDTH), lambda b, g, t: (b * nc + order(t), g)),
                pl.BlockSpec((L, SSM_STATE), lambda b, g, t: (b * nc + order(t), b_col + g)),
                pl.BlockSpec((L, SSM_STATE), lambda b, g, t: (b * nc + order(t), c_col + g)),
                pl.BlockSpec((L, LANES), lambda b, g, t: (b * nc + order(t), g))]

    small = pl.BlockSpec((1, LANES), lambda b, g, t: (0, g))
    y_shape = jax.ShapeDtypeStruct((lay.m, D_MODEL), F32)
    return pl.pallas_call(
        _ssd_kernel,
        out_shape=(y_shape, y_shape),
        grid=(lay.batch, SSM_GROUPS, nc),
        in_specs=specs(fwd) + specs(rev) + [small, small],
        out_specs=(pl.BlockSpec((L, GROUP_WIDTH), lambda b, g, t: (b * nc + fwd(t), g)),
                   pl.BlockSpec((L, GROUP_WIDTH), lambda b, g, t: (b * nc + rev(t), g))),
        scratch_shapes=[pltpu.VMEM((SSM_STATE, GROUP_WIDTH), F32), pltpu.VMEM((SSM_STATE, GROUP_WIDTH), F32)],
        compiler_params=_params(("parallel", "parallel", "arbitrary"), 8 * L * GROUP_WIDTH * 4),
        name="ssd_scan",
    )(xbc, xbc, xbc, dt_raw, xbc, xbc, xbc, dt_raw, dt_bias_g, a_log_g)


def _ssd_out_kernel(yf_ref, yr_ref, x_ref, z_ref, d_ref, g_ref, o_ref):
    y = d_ref[...] * x_ref[...] + yf_ref[...] + yr_ref[...]
    yz = y * _silu(z_ref[...])
    for g in range(SSM_GROUPS):
        cols = slice(g * GROUP_WIDTH, (g + 1) * GROUP_WIDTH)
        v = yz[:, cols]
        inv = lax.rsqrt(jnp.mean(v * v, axis=-1, keepdims=True) + RMS_EPS)
        o_ref[:, cols] = (v * inv * g_ref[:, cols]).astype(o_ref.dtype)


def _ssd_out(lay, y_f, y_r, xbc, zxbc, d_exp, norm_g):
    row = pl.BlockSpec((TILE, D_MODEL), lambda i: (i, 0))
    vec = pl.BlockSpec((1, D_MODEL), lambda i: (0, 0))
    return pl.pallas_call(
        _ssd_out_kernel,
        out_shape=jax.ShapeDtypeStruct((lay.m, D_MODEL), BF16),
        grid=(lay.n_tiles,),
        in_specs=[row, row, row, row, vec, vec],
        out_specs=row,
        compiler_params=_params(("parallel",), 6 * TILE * D_MODEL * 4),
        name="ssd_gated_norm",
    )(y_f, y_r, xbc, zxbc, d_exp, norm_g)


def _rope_tables(seq):
    half = ATT_HEAD_DIM // 2
    nf = half // 2
    inv = ROPE_BASE ** (-(jnp.arange(nf, dtype=F32) / nf))
    rows = seq // GRID_W
    row = jnp.repeat(jnp.arange(rows, dtype=F32), GRID_W)
    col = jnp.tile(jnp.arange(GRID_W, dtype=F32), rows)
    lane = jnp.arange(2 * ATT_HEAD_DIM)
    freq = inv[lane % nf]
    pos = jnp.where((lane % ATT_HEAD_DIM) < half, row[:, None], col[:, None])
    ang = pos * freq[None, :]
    low = (lane % half) < nf
    sin = jnp.sin(ang)
    return jnp.cos(ang), jnp.where(low, -sin, 0.0), jnp.where(low, 0.0, sin)


def _rope(t, cos, sin_lo, sin_hi):
    nf = ATT_HEAD_DIM // 4
    return t * cos + pltpu.roll(t, LANES - nf, axis=1) * sin_lo + pltpu.roll(t, nf, axis=1) * sin_hi


def _dot_nt(a, b):
    return lax.dot_general(a, b, (((1,), (1,)), ((), ())), preferred_element_type=F32)


def _attn_kernel(q_ref, k_ref, v_ref, cos_ref, slo_ref, shi_ref, lam_ref, g_ref, o_ref,
                 krot_ref, kctx_ref, vb_ref, *, ctx, ctx_tiles, lam_init):
    qi = pl.program_id(2)

    @pl.when(qi == 0)
    def _():
        krot_ref[...] = _rope(k_ref[ctx:, :], cos_ref[...], slo_ref[...], shi_ref[...]).astype(BF16)
        kctx_ref[...] = k_ref[:ctx, :].astype(BF16)
        vb_ref[...] = v_ref[...].astype(BF16)

    lp = lam_ref[...]
    lam = (jnp.exp(jnp.sum(lp[0:1] * lp[1:2], axis=-1, keepdims=True))
           - jnp.exp(jnp.sum(lp[2:3] * lp[3:4], axis=-1, keepdims=True)) + lam_init)
    first_map = lax.broadcasted_iota(jnp.int32, (TILE, LANES), 1) < ATT_HEAD_DIM
    qn = q_ref[...] * (ATT_HEAD_DIM ** -0.5)

    def split(q):
        return jnp.where(first_map, q, 0.0).astype(BF16), jnp.where(first_map, 0.0, q).astype(BF16)

    def finish(o):
        inv = lax.rsqrt(jnp.mean(o * o, axis=-1, keepdims=True) + RMS_EPS)
        o_ref[...] = (o * inv * g_ref[...] * (1.0 - lam_init)).astype(o_ref.dtype)

    @pl.when(qi < ctx_tiles)
    def _():
        probs = []
        for qm in split(qn):
            sc = _dot_nt(qm, kctx_ref[...])
            p = jnp.exp(sc - jnp.max(sc, axis=-1, keepdims=True))
            probs.append(p * (1.0 / jnp.sum(p, axis=-1, keepdims=True)))
        pd = (probs[0] - lam * probs[1]).astype(BF16)
        finish(jnp.dot(pd, vb_ref[:ctx, :], preferred_element_type=F32))

    @pl.when(qi >= ctx_tiles)
    def _():
        start = pl.multiple_of((qi - ctx_tiles) * TILE, TILE)
        qr = _rope(qn, cos_ref[pl.ds(start, TILE), :], slo_ref[pl.ds(start, TILE), :], shi_ref[pl.ds(start, TILE), :])
        pc, pl_ = [], []
        for qm_n, qm_r in zip(split(qn), split(qr)):
            sc = _dot_nt(qm_n, kctx_ref[...])
            sl = _dot_nt(qm_r, krot_ref[...])
            mx = jnp.maximum(jnp.max(sc, axis=-1, keepdims=True), jnp.max(sl, axis=-1, keepdims=True))
            ec = jnp.exp(sc - mx)
            el = jnp.exp(sl - mx)
            inv = 1.0 / (jnp.sum(ec, axis=-1, keepdims=True) + jnp.sum(el, axis=-1, keepdims=True))
            pc.append(ec * inv)
            pl_.append(el * inv)
        pd_c = (pc[0] - lam * pc[1]).astype(BF16)
        pd_l = (pl_[0] - lam * pl_[1]).astype(BF16)
        finish(jnp.dot(pd_c, vb_ref[:ctx, :], preferred_element_type=F32)
               + jnp.dot(pd_l, vb_ref[ctx:, :], preferred_element_type=F32))


def _attention(lay, qkv, tables, lam_p, subln_g, lam_init):
    hb = D_MODEL // LANES
    tab = pl.BlockSpec((lay.seq, LANES), lambda b, h, i: (0, 0))
    kv_bytes = lay.nt * LANES * 4
    return pl.pallas_call(
        functools.partial(_attn_kernel, ctx=lay.ctx, ctx_tiles=lay.ctx_tiles, lam_init=lam_init),
        out_shape=jax.ShapeDtypeStruct((lay.m, D_MODEL), BF16),
        grid=(lay.batch, ATT_HEADS, lay.tiles_per_batch),
        in_specs=[pl.BlockSpec((TILE, LANES), lambda b, h, i: (b * lay.tiles_per_batch + i, h)),
                  pl.BlockSpec((lay.nt, LANES), lambda b, h, i: (b, hb + h)),
                  pl.BlockSpec((lay.nt, LANES), lambda b, h, i: (b, 2 * hb + h)),
                  tab, tab, tab,
                  pl.BlockSpec((8, ATT_HEAD_DIM), lambda b, h, i: (0, 0)),
                  pl.BlockSpec((1, LANES), lambda b, h, i: (0, 0))],
        out_specs=pl.BlockSpec((TILE, LANES), lambda b, h, i: (b * lay.tiles_per_batch + i, h)),
        scratch_shapes=[pltpu.VMEM((lay.seq, LANES), BF16), pltpu.VMEM((lay.ctx, LANES), BF16),
                        pltpu.VMEM((lay.nt, LANES), BF16)],
        compiler_params=_params(("parallel", "parallel", "arbitrary"),
                                2 * kv_bytes + 3 * lay.seq * LANES * 4 + 4 * TILE * lay.nt * 4),
        name="diff_attention",
    )(qkv, qkv, qkv, *tables, lam_p, subln_g)


def _merge_kernel(ys_ref, ya_ref, ws_ref, wa_ref, gs_ref, ga_ref, o_ref):
    ps = jnp.dot(ys_ref[...], ws_ref[...], preferred_element_type=F32)
    pa = jnp.dot(ya_ref[...], wa_ref[...], preferred_element_type=F32)
    o_ref[...] = (jax.nn.sigmoid(gs_ref[...]) * ps + jax.nn.sigmoid(ga_ref[...]) * pa).astype(o_ref.dtype)


def _merge(lay, y_s, y_a, w_s, w_a, gates):
    tm, tn = 512, 1024
    goff = D_MODEL // tn
    a_spec = pl.BlockSpec((tm, D_MODEL), lambda i, j: (i, 0))
    w_spec = pl.BlockSpec((D_MODEL, tn), lambda i, j: (0, j))
    return pl.pallas_call(
        _merge_kernel,
        out_shape=jax.ShapeDtypeStruct((lay.m, D_MODEL), BF16),
        grid=(lay.m // tm, D_MODEL // tn),
        in_specs=[a_spec, a_spec, w_spec, w_spec,
                  pl.BlockSpec((tm, tn), lambda i, j: (i, j)),
                  pl.BlockSpec((tm, tn), lambda i, j: (i, goff + j))],
        out_specs=pl.BlockSpec((tm, tn), lambda i, j: (i, j)),
        compiler_params=_params(("parallel", "parallel"),
                                2 * tm * D_MODEL * 2 + 2 * D_MODEL * tn * 2 + 3 * tm * tn * 4),
        name="branch_merge",
    )(y_s, y_a, w_s, w_a, gates, gates)


def _outproj_kernel(m_ref, w_ref, h_ref, gate_ref, lg_ref, lb_ref, sh_ref, sc_ref, h_out, u_out, *, alpha):
    o = jnp.dot(m_ref[...], w_ref[...], preferred_element_type=F32)
    h1 = _ln(alpha * h_ref[...] + gate_ref[...] * o) * lg_ref[...] + lb_ref[...]
    h_out[...] = h1
    u_out[...] = _ln(h1) * (1.0 + sc_ref[...]) + sh_ref[...]


def _outproj(lay, m, w_out, h, mod3, ln_g, ln_b, alpha):
    row = pl.BlockSpec((TILE, D_MODEL), lambda i: (i, 0))
    vec = pl.BlockSpec((1, D_MODEL), lambda i: (0, 0))
    shape = jax.ShapeDtypeStruct((lay.m, D_MODEL), F32)
    return pl.pallas_call(
        functools.partial(_outproj_kernel, alpha=alpha),
        out_shape=(shape, shape),
        grid=(lay.n_tiles,),
        in_specs=[row, pl.BlockSpec((D_MODEL, D_MODEL), lambda i: (0, 0)), row,
                  lay.mod_spec(2), vec, vec, lay.mod_spec(3), lay.mod_spec(4)],
        out_specs=(row, row),
        compiler_params=_params(("parallel",), D_MODEL * D_MODEL * 2 + 5 * TILE * D_MODEL * 4),
        name="out_proj_residual",
    )(m, w_out, h, mod3, ln_g, ln_b, mod3, mod3)


def _router_kernel(u_ref, w_ref, b_ref, ids_ref, wts_ref):
    logits = jnp.dot(u_ref[...], w_ref[...], precision=HIGHEST, preferred_element_type=F32) + b_ref[...]
    lane_i = lax.broadcasted_iota(jnp.int32, logits.shape, 1)
    lane = lane_i.astype(F32)
    none = float(LANES)
    neg = -jnp.inf

    def softmax(mask):
        v = jnp.where(mask, logits, neg)
        e = jnp.exp(v - jnp.max(v, axis=-1, keepdims=True))
        return e / jnp.sum(e, axis=-1, keepdims=True)

    def top(p, mask):
        best = jnp.max(jnp.where(mask, p, -1.0), axis=-1, keepdims=True)
        idx = jnp.min(jnp.where(mask & (p == best), lane, none), axis=-1, keepdims=True)
        return best, idx

    is_group = lane_i < MOE_GROUPS
    p_grp, grp = top(softmax(is_group), is_group)
    first = MOE_GROUPS + grp * EXPERTS_PER_GROUP
    in_grp = (lane >= first) & (lane < first + EXPERTS_PER_GROUP)
    pe = softmax(in_grp)
    p1, i1 = top(pe, in_grp)
    rest = in_grp & (lane != i1)
    p2, i2 = top(pe, rest)
    denom = p1 + p2
    w1 = p_grp * p1 / denom
    w2 = p_grp * p2 / denom
    ids = jnp.where(lane_i == 0, i1 - MOE_GROUPS, jnp.where(lane_i == 1, i2 - MOE_GROUPS, 0.0))
    ids_ref[...] = ids.astype(jnp.int32)
    wts_ref[...] = jnp.where(lane_i == 0, w1, jnp.where(lane_i == 1, w2, 0.0))


def _router(lay, u, w_r, b_r):
    row = pl.BlockSpec((TILE, D_MODEL), lambda i: (i, 0))
    out = pl.BlockSpec((TILE, LANES), lambda i: (i, 0))
    return pl.pallas_call(
        _router_kernel,
        out_shape=(jax.ShapeDtypeStruct((lay.m, LANES), jnp.int32), jax.ShapeDtypeStruct((lay.m, LANES), F32)),
        grid=(lay.n_tiles,),
        in_specs=[row, pl.BlockSpec((D_MODEL, LANES), lambda i: (0, 0)), pl.BlockSpec((1, LANES), lambda i: (0, 0))],
        out_specs=(out, out),
        compiler_params=_params(("parallel",), TILE * D_MODEL * 4 + D_MODEL * LANES * 4),
        name="moe_router",
    )(u, w_r, b_r)


def _dispatch(ids, n_tok):
    e_flat = ids[:, :MOE_TOP_K].reshape(-1)
    n_assign = n_tok * MOE_TOP_K
    onehot = (e_flat[:, None] == jnp.arange(N_EXPERTS, dtype=jnp.int32)[None, :]).astype(jnp.int32)
    csum = jnp.cumsum(onehot, axis=0)
    rank = jnp.sum(csum * onehot, axis=1) - 1
    counts = csum[-1]
    padded = (counts + MOE_BLOCK - 1) // MOE_BLOCK * MOE_BLOCK
    pend = jnp.cumsum(padded)
    pstarts = pend - padded
    dest = (pstarts[e_flat] + rank).astype(jnp.int32)
    n_blocks = n_assign // MOE_BLOCK + N_EXPERTS
    tok_flat = jnp.arange(n_assign, dtype=jnp.int32) // MOE_TOP_K
    slot_tok = jnp.zeros((n_blocks * MOE_BLOCK,), jnp.int32).at[dest].set(tok_flat)
    block_start = jnp.arange(n_blocks, dtype=jnp.int32) * MOE_BLOCK
    block_exp = jnp.minimum(jnp.sum(block_start[:, None] >= pend[None, :], axis=1), N_EXPERTS - 1).astype(jnp.int32)
    n_used = (pend[-1] // MOE_BLOCK).astype(jnp.int32).reshape(1)
    return dest, slot_tok, block_exp, n_used, n_blocks


def _row_copy(src_hbm, src_row, dst, dst_row, sem):
    return pltpu.make_async_copy(src_hbm.at[pl.ds(src_row, 1)], dst.at[pl.ds(dst_row, 1)], sem)


def _expert_kernel(bexp_ref, stok_ref, nused_ref, u_hbm, wg_ref, wu_ref, wd_ref, o_ref, xbuf, sem):
    i = pl.program_id(0)
    n = nused_ref[0]
    slot = i % 2

    def gather(blk, s, start):
        base = blk * MOE_BLOCK

        def body(r, carry):
            cp = _row_copy(u_hbm, stok_ref[base + r], xbuf.at[s], r, sem.at[s])
            if start:
                cp.start()
            else:
                cp.wait()
            return carry

        lax.fori_loop(0, MOE_BLOCK, body, 0)

    @pl.when(i == 0)
    def _():
        gather(0, 0, True)

    @pl.when(i + 1 < n)
    def _():
        gather(i + 1, 1 - slot, True)

    @pl.when(i < n)
    def _():
        gather(i, slot, False)
        x = xbuf[slot].astype(BF16)
        hid = _silu(jnp.dot(x, wg_ref[...], preferred_element_type=F32)) * jnp.dot(x, wu_ref[...], preferred_element_type=F32)
        o_ref[...] = jnp.dot(hid.astype(BF16), wd_ref[...], preferred_element_type=F32)

    @pl.when(i >= n)
    def _():
        o_ref[...] = jnp.zeros_like(o_ref)


def _experts(u, block_exp, slot_tok, n_used, n_blocks, w_gate, w_up, w_down):
    wspec_in = pl.BlockSpec((None, D_MODEL, EXPERT_FF), lambda i, be, st, nu: (be[i], 0, 0))
    grid_spec = pltpu.PrefetchScalarGridSpec(
        num_scalar_prefetch=3,
        grid=(n_blocks,),
        in_specs=[pl.BlockSpec(memory_space=pl.ANY), wspec_in, wspec_in,
                  pl.BlockSpec((None, EXPERT_FF, D_MODEL), lambda i, be, st, nu: (be[i], 0, 0))],
        out_specs=pl.BlockSpec((MOE_BLOCK, D_MODEL), lambda i, be, st, nu: (i, 0)),
        scratch_shapes=[pltpu.VMEM((2, MOE_BLOCK, D_MODEL), F32), pltpu.SemaphoreType.DMA((2,))],
    )
    return pl.pallas_call(
        _expert_kernel,
        out_shape=jax.ShapeDtypeStruct((n_blocks * MOE_BLOCK, D_MODEL), F32),
        grid_spec=grid_spec,
        compiler_params=_params(("arbitrary",), 3 * D_MODEL * EXPERT_FF * 2 + 2 * MOE_BLOCK * D_MODEL * 4),
        name="moe_experts",
    )(block_exp, slot_tok, n_used, u, w_gate, w_up, w_down)


def _combine_kernel(dest_ref, y_hbm, wts_ref, h_ref, gate_ref, lg_ref, lb_ref, sh_ref, sc_ref, h_out, u_out,
                    ybuf, sem, *, alpha, n_tiles):
    i = pl.program_id(0)
    slot = i % 2
    rows = TILE * MOE_TOP_K

    def gather(tile, s, start):
        base = tile * rows

        def body(r, carry):
            cp = _row_copy(y_hbm, dest_ref[base + r], ybuf.at[s], (r % MOE_TOP_K) * TILE + r // MOE_TOP_K, sem.at[s])
            if start:
                cp.start()
            else:
                cp.wait()
            return carry

        lax.fori_loop(0, rows, body, 0)

    @pl.when(i == 0)
    def _():
        gather(0, 0, True)

    @pl.when(i + 1 < n_tiles)
    def _():
        gather(i + 1, 1 - slot, True)

    gather(i, slot, False)
    wts = wts_ref[...]
    y = wts[:, 0:1] * ybuf[slot, 0:TILE, :] + wts[:, 1:2] * ybuf[slot, TILE:2 * TILE, :]
    h2 = _ln(alpha * h_ref[...] + gate_ref[...] * y) * lg_ref[...] + lb_ref[...]
    h_out[...] = h2
    u_out[...] = (_ln(h2) * (1.0 + sc_ref[...]) + sh_ref[...]).astype(u_out.dtype)


def _combine(lay, dest, y_slots, wts, h, mod3, mod3_next, ln_g, ln_b, alpha):
    row = pl.BlockSpec((TILE, D_MODEL), lambda i, d: (i, 0))
    vec = pl.BlockSpec((1, D_MODEL), lambda i, d: (0, 0))
    grid_spec = pltpu.PrefetchScalarGridSpec(
        num_scalar_prefetch=1,
        grid=(lay.n_tiles,),
        in_specs=[pl.BlockSpec(memory_space=pl.ANY), pl.BlockSpec((TILE, LANES), lambda i, d: (i, 0)), row,
                  lay.mod_spec(5), vec, vec, lay.mod_spec(0), lay.mod_spec(1)],
        out_specs=(row, row),
        scratch_shapes=[pltpu.VMEM((2, MOE_TOP_K * TILE, D_MODEL), F32), pltpu.SemaphoreType.DMA((2,))],
    )
    return pl.pallas_call(
        functools.partial(_combine_kernel, alpha=alpha, n_tiles=lay.n_tiles),
        out_shape=(jax.ShapeDtypeStruct((lay.m, D_MODEL), F32), jax.ShapeDtypeStruct((lay.m, D_MODEL), BF16)),
        grid_spec=grid_spec,
        compiler_params=_params(("arbitrary",), 5 * TILE * D_MODEL * 4),
        name="moe_combine_residual",
    )(dest, y_slots, wts, h, mod3, ln_g, ln_b, mod3_next, mod3_next)


def _group_lanes(v):
    per_group = v.reshape(2, SSM_GROUPS, HEADS_PER_GROUP).transpose(1, 0, 2).reshape(SSM_GROUPS, 2 * HEADS_PER_GROUP)
    return jnp.pad(per_group, ((0, 0), (0, LANES - 2 * HEADS_PER_GROUP))).reshape(1, SSM_GROUPS * LANES)


def kernel(x, c, ctx, c_ctx, w_ada, b_ada, w_in, conv_w, conv_b, ssm_dt_bias, ssm_a_log, ssm_d, ssm_norm_g, lam_q1, lam_k1, lam_q2, lam_k2, attn_subln_g, w_br_ssm, w_br_att, w_out, ln1_g, ln1_b, w_router_group, b_router_group, w_router_expert, b_router_expert, w_exp_gate, w_exp_up, w_exp_down, ln2_g, ln2_b):
    batch, seq, d = x.shape
    depth = w_ada.shape[0]
    assert d == D_MODEL and batch < MOD_ROWS
    lay = _Layout(batch, seq, ctx.shape[1])
    alpha = (2.0 * depth) ** 0.25

    c_all = jnp.zeros((MOD_ROWS, d), F32).at[:batch].set(c).at[batch].set(c_ctx)
    mod = _adaln(c_all, w_ada, b_ada).reshape(depth, MOD_ROWS, 1, 6 * d)
    tables = _rope_tables(seq)

    o_dt = d + CONV_DIM
    o_q = o_dt + 2 * SSM_HEADS
    o_g = o_q + 3 * d

    h = jnp.concatenate([ctx, x], axis=1).reshape(lay.m, d)
    u = _lnmod(lay, h, mod[0], 0, 1, BF16)
    for l in range(depth):
        lam_init = 0.8 - 0.6 * math.exp(-0.3 * l)
        w = w_in[l]
        w_dt = w[:, o_dt:o_q].reshape(d, 2, SSM_GROUPS, HEADS_PER_GROUP).transpose(0, 2, 1, 3)
        w_dt = jnp.pad(w_dt.reshape(d, SSM_GROUPS, 2 * HEADS_PER_GROUP),
                       ((0, 0), (0, 0), (0, LANES - 2 * HEADS_PER_GROUP))).reshape(d, SSM_GROUPS * LANES)
        zxbc = _matmul(u, w[:, :o_dt].astype(BF16))
        dt_raw = _matmul(u, w_dt.astype(BF16))
        qkv = _matmul(u, w[:, o_q:o_g].astype(BF16))
        gates = _matmul(u, w[:, o_g:].astype(BF16))

        xbc = _conv_silu(lay, zxbc, conv_w[l], conv_b[l])
        y_f, y_r = _ssd(lay, xbc, dt_raw, _group_lanes(ssm_dt_bias[l]), _group_lanes(ssm_a_log[l]))
        y_s = _ssd_out(lay, y_f, y_r, xbc, zxbc, jnp.repeat(ssm_d[l], SSM_HEAD_DIM).reshape(1, d),
                       ssm_norm_g[l].reshape(1, d))

        lam_p = jnp.zeros((8, ATT_HEAD_DIM), F32).at[0].set(lam_q1[l]).at[1].set(lam_k1[l]).at[2].set(lam_q2[l]).at[3].set(lam_k2[l])
        y_a = _attention(lay, qkv, tables, lam_p, attn_subln_g[l].reshape(1, LANES), lam_init)

        m = _merge(lay, y_s, y_a, w_br_ssm[l].astype(BF16), w_br_att[l].astype(BF16), gates)
        h, u2 = _outproj(lay, m, w_out[l].astype(BF16), h, mod[l], ln1_g[l].reshape(1, d), ln1_b[l].reshape(1, d), alpha)

        w_r = jnp.zeros((d, LANES), F32).at[:, :MOE_GROUPS].set(w_router_group[l]).at[:, MOE_GROUPS:MOE_GROUPS + N_EXPERTS].set(w_router_expert[l])
        b_r = jnp.zeros((1, LANES), F32).at[0, :MOE_GROUPS].set(b_router_group[l]).at[0, MOE_GROUPS:MOE_GROUPS + N_EXPERTS].set(b_router_expert[l])
        ids, wts = _router(lay, u2, w_r, b_r)
        dest, slot_tok, block_exp, n_used, n_blocks = _dispatch(ids, lay.m)
        y_slots = _experts(u2, block_exp, slot_tok, n_used, n_blocks,
                           w_exp_gate[l].astype(BF16), w_exp_up[l].astype(BF16), w_exp_down[l].astype(BF16))
        h, u = _combine(lay, dest, y_slots, wts, h, mod[l], mod[min(l + 1, depth - 1)],
                        ln2_g[l].reshape(1, d), ln2_b[l].reshape(1, d), alpha)
    return h.reshape(batch, lay.nt, d)[:, lay.ctx:]
```

```python
import functools
import math

import jax
import jax.numpy as jnp
from jax import lax
from jax.experimental import pallas as pl
from jax.experimental.pallas import tpu as pltpu

F32 = jnp.float32
BF16 = jnp.bfloat16
HIGHEST = lax.Precision.HIGHEST

D_MODEL = 2048
GRID_W = 64

SSM_HEAD_DIM = 64
SSM_HEADS = D_MODEL // SSM_HEAD_DIM
SSM_GROUPS = 4
SSM_STATE = 128
HEADS_PER_GROUP = SSM_HEADS // SSM_GROUPS
GROUP_WIDTH = HEADS_PER_GROUP * SSM_HEAD_DIM
CONV_WIDTH = 5
CONV_DIM = D_MODEL + 2 * SSM_GROUPS * SSM_STATE
SSD_CHUNK = 128

ATT_HEAD_DIM = 64
ATT_HEADS = D_MODEL // (2 * ATT_HEAD_DIM)
ROPE_BASE = 10000.0
ATT_HEAD_BLOCK = 2
ATT_Q_SCALE = ATT_HEAD_DIM ** -0.5 * math.log2(math.e)

MOE_GROUPS = 4
EXPERTS_PER_GROUP = 8
N_EXPERTS = MOE_GROUPS * EXPERTS_PER_GROUP
MOE_TOP_K = 2
EXPERT_FF = D_MODEL // 2

LN_EPS = 1e-6
RMS_EPS = 1e-5

LANES = 128
MOD_ROWS = 16
TILE = 256
MOE_BLOCK = 256
VMEM_CAP = 60 * 1024 * 1024
VMEM_SLACK = 6 * 1024 * 1024


def _params(sem, block_bytes):
    return pltpu.CompilerParams(dimension_semantics=sem,
                                vmem_limit_bytes=int(min(VMEM_CAP, 2 * block_bytes + VMEM_SLACK)))


def _ln(x):
    mu = jnp.mean(x, axis=-1, keepdims=True)
    xc = x - mu
    var = jnp.mean(xc * xc, axis=-1, keepdims=True)
    return xc * lax.rsqrt(var + LN_EPS)


def _silu(x):
    return x * jax.nn.sigmoid(x)


def _mod_kernel(c_ref, w_ref, b_ref, o_ref):
    sc = _silu(c_ref[...]).astype(BF16)
    o_ref[...] = jnp.dot(sc, w_ref[...].astype(BF16), preferred_element_type=F32) + b_ref[...]


def _adaln(c_all, w_ada, b_ada):
    depth, d, n = w_ada.shape
    tn = 1024
    return pl.pallas_call(
        _mod_kernel,
        out_shape=jax.ShapeDtypeStruct((depth, MOD_ROWS, n), F32),
        grid=(depth, n // tn),
        in_specs=[pl.BlockSpec((MOD_ROWS, d), lambda l, j: (0, 0)),
                  pl.BlockSpec((None, d, tn), lambda l, j: (l, 0, j)),
                  pl.BlockSpec((None, 1, tn), lambda l, j: (l, 0, j))],
        out_specs=pl.BlockSpec((None, MOD_ROWS, tn), lambda l, j: (l, 0, j)),
        compiler_params=_params(("parallel", "parallel"), d * tn * 4 + d * tn * 2),
        name="adaln_mod",
    )(c_all, w_ada, b_ada.reshape(depth, 1, n))


class _Layout:
    def __init__(self, batch, seq, ctx):
        assert ctx % TILE == 0 and seq % TILE == 0 and seq % GRID_W == 0
        self.batch, self.seq, self.ctx = batch, seq, ctx
        self.nt = ctx + seq
        self.m = batch * self.nt
        self.tiles_per_batch = self.nt // TILE
        self.ctx_tiles = ctx // TILE
        self.n_tiles = self.m // TILE

    def mod_row(self, i):
        return jnp.where(i % self.tiles_per_batch < self.ctx_tiles, self.batch, i // self.tiles_per_batch)

    def mod_spec(self, chunk):
        return pl.BlockSpec((None, 1, D_MODEL), lambda i, *_: (self.mod_row(i), 0, chunk))


def _lnmod_kernel(h_ref, sh_ref, sc_ref, u_ref):
    u_ref[...] = (_ln(h_ref[...]) * (1.0 + sc_ref[...]) + sh_ref[...]).astype(u_ref.dtype)


def _lnmod(lay, h, mod3, shift_chunk, scale_chunk, out_dtype):
    row = pl.BlockSpec((TILE, D_MODEL), lambda i: (i, 0))
    return pl.pallas_call(
        _lnmod_kernel,
        out_shape=jax.ShapeDtypeStruct((lay.m, D_MODEL), out_dtype),
        grid=(lay.n_tiles,),
        in_specs=[row, lay.mod_spec(shift_chunk), lay.mod_spec(scale_chunk)],
        out_specs=row,
        compiler_params=_params(("parallel",), 2 * TILE * D_MODEL * 4),
        name="ln_modulate",
    )(h, mod3, mod3)


def _mm_kernel(a_ref, w_ref, o_ref):
    o_ref[...] = jnp.dot(a_ref[...], w_ref[...], preferred_element_type=F32).astype(o_ref.dtype)


def _matmul(a, w, layer, out_dtype=F32, tm=1024, tn=1024):
    m, k = a.shape
    n = w.shape[2]
    tn = min(tn, n)
    tm = math.gcd(m, tm)
    assert tm % TILE == 0 and n % tn == 0
    return pl.pallas_call(
        _mm_kernel,
        out_shape=jax.ShapeDtypeStruct((m, n), out_dtype),
        grid=(m // tm, n // tn),
        in_specs=[pl.BlockSpec((tm, k), lambda i, j: (i, 0)),
                  pl.BlockSpec((None, k, tn), lambda i, j: (layer, 0, j))],
        out_specs=pl.BlockSpec((tm, tn), lambda i, j: (i, j)),
        compiler_params=_params(("parallel", "parallel"), tm * k * 2 + k * tn * 2 + tm * tn * 4),
        name="matmul",
    )(a, w)


IN_DT = D_MODEL + CONV_DIM
IN_Q = IN_DT + 2 * SSM_HEADS
PROJ_Q = IN_DT
PROJ_GATES = PROJ_Q + 3 * D_MODEL
PROJ_COLS = PROJ_GATES + 2 * D_MODEL
PACK_TN = 512


def _pack_kernel(a_ref, b_ref, o_ref, *, aligned_blocks, shift):
    j = pl.program_id(1)

    @pl.when(j < aligned_blocks)
    def _():
        o_ref[...] = a_ref[...].astype(BF16)

    @pl.when(j >= aligned_blocks)
    def _():
        o_ref[...] = jnp.concatenate([a_ref[:, shift:], b_ref[:, :shift]], axis=1).astype(BF16)


def _pack_w_in(w_in):
    depth, d, _ = w_in.shape
    shift = IN_Q - PROJ_Q
    return pl.pallas_call(
        functools.partial(_pack_kernel, aligned_blocks=PROJ_Q // PACK_TN, shift=shift),
        out_shape=jax.ShapeDtypeStruct((depth, d, PROJ_COLS), BF16),
        grid=(depth, PROJ_COLS // PACK_TN),
        in_specs=[pl.BlockSpec((None, d, PACK_TN), lambda l, j: (l, 0, j)),
                  pl.BlockSpec((None, d, PACK_TN), lambda l, j: (l, 0, j + 1))],
        out_specs=pl.BlockSpec((None, d, PACK_TN), lambda l, j: (l, 0, j)),
        compiler_params=_params(("parallel", "parallel"), 3 * d * PACK_TN * 4),
        name="pack_w_in",
    )(w_in, w_in)


def _conv_kernel(x_ref, w_ref, b_ref, o_ref, *, ctx):
    x = x_ref[...]
    nt = x.shape[0]
    t = lax.broadcasted_iota(jnp.int32, x.shape, 0)
    lo = jnp.where(t < ctx, 0, ctx)
    hi = jnp.where(t < ctx, ctx, nt)
    pad = (CONV_WIDTH - 1) // 2
    acc = b_ref[...] + x * w_ref[pad:pad + 1, :]
    for k in range(CONV_WIDTH):
        d = k - pad
        if d == 0:
            continue
        shifted = pltpu.roll(x, (-d) % nt, axis=0)
        valid = (t + d >= lo) & (t + d < hi)
        acc = acc + jnp.where(valid, shifted, 0.0) * w_ref[k:k + 1, :]
    o_ref[...] = _silu(acc)


def _conv_silu(lay, zxbc, conv_w, conv_b):
    tc = 512
    off = D_MODEL // tc
    return pl.pallas_call(
        functools.partial(_conv_kernel, ctx=lay.ctx),
        out_shape=jax.ShapeDtypeStruct((lay.m, CONV_DIM), F32),
        grid=(lay.batch, CONV_DIM // tc),
        in_specs=[pl.BlockSpec((lay.nt, tc), lambda b, j: (b, off + j)),
                  pl.BlockSpec((CONV_WIDTH, tc), lambda b, j: (0, j)),
                  pl.BlockSpec((1, tc), lambda b, j: (0, j))],
        out_specs=pl.BlockSpec((lay.nt, tc), lambda b, j: (b, j)),
        compiler_params=_params(("parallel", "parallel"), 6 * lay.nt * tc * 4),
        name="conv_silu",
    )(zxbc, conv_w, conv_b.reshape(1, CONV_DIM))


def _ssd_direction(x_ref, b_ref, c_ref, dt_ref, bias_ref, alog_ref, st_ref, y_ref, *, lane_off, upper):
    L = SSD_CHUNK
    r = lax.broadcasted_iota(jnp.int32, (L, L), 0)
    s = lax.broadcasted_iota(jnp.int32, (L, L), 1)
    tri = (s >= r) if upper else (s <= r)
    x = x_ref[...]
    bm = b_ref[...]
    cm = c_ref[...].astype(BF16)
    dtv = dt_ref[...] + bias_ref[...]
    dt = jnp.maximum(dtv, 0.0) + jnp.log1p(jnp.exp(-jnp.abs(dtv)))
    adt = dt * (-jnp.exp(alog_ref[...]))
    cs = jnp.dot(tri.astype(F32), adt, precision=HIGHEST, preferred_element_type=F32)
    cs_t = cs.T
    a_tot = jnp.sum(adt, axis=0, keepdims=True)
    cb = lax.dot_general(cm, bm.astype(BF16), (((1,), (1,)), ((), ())), preferred_element_type=F32)
    y_off = jnp.dot(cm, st_ref[...].astype(BF16), preferred_element_type=F32)
    bm_t = bm.T.astype(BF16)
    first_half = lax.broadcasted_iota(jnp.int32, (L, LANES), 1) < SSM_HEAD_DIM
    first_half_row = first_half[0:1, :]
    for pair in range(HEADS_PER_GROUP // 2):
        la = lane_off + 2 * pair
        lb = la + 1
        cols = slice(pair * LANES, (pair + 1) * LANES)
        cs_pair = jnp.where(first_half, cs[:, la:la + 1], cs[:, lb:lb + 1])
        dt_pair = jnp.where(first_half, dt[:, la:la + 1], dt[:, lb:lb + 1])
        tot_pair = jnp.where(first_half_row, a_tot[:, la:la + 1], a_tot[:, lb:lb + 1])
        xdt = x[:, cols] * dt_pair
        y_pair = y_off[:, cols] * jnp.exp(cs_pair)
        for lane, keep in ((la, first_half), (lb, jnp.logical_not(first_half))):
            seg = cs[:, lane:lane + 1] - cs_t[lane:lane + 1, :]
            decay = jnp.where(tri, jnp.exp(jnp.where(tri, seg, 0.0)), 0.0)
            mh = (cb * decay).astype(BF16)
            y_pair = y_pair + jnp.dot(mh, jnp.where(keep, xdt, 0.0).astype(BF16), preferred_element_type=F32)
        y_ref[:, cols] = y_pair
        z = (xdt * jnp.exp(tot_pair - cs_pair)).astype(BF16)
        st_ref[:, cols] = st_ref[:, cols] * jnp.exp(tot_pair) + jnp.dot(bm_t, z, preferred_element_type=F32)


def _ssd_kernel(xf, bf, cf, dtf, xr, br, cr, dtr, bias_ref, alog_ref, yf_ref, yr_ref, stf_ref, str_ref):
    @pl.when(pl.program_id(2) == 0)
    def _():
        stf_ref[...] = jnp.zeros_like(stf_ref)
        str_ref[...] = jnp.zeros_like(str_ref)

    _ssd_direction(xf, bf, cf, dtf, bias_ref, alog_ref, stf_ref, yf_ref, lane_off=0, upper=False)
    _ssd_direction(xr, br, cr, dtr, bias_ref, alog_ref, str_ref, yr_ref, lane_off=HEADS_PER_GROUP, upper=True)


def _ssd(lay, xbc, dt_raw, dt_bias_g, a_log_g):
    L = SSD_CHUNK
    nc = lay.nt // L
    ctx_chunks = lay.ctx // L
    b_col = D_MODEL // SSM_STATE
    c_col = b_col + SSM_GROUPS

    def fwd(t):
        return t

    def rev(t):
        return jnp.where(t < ctx_chunks, ctx_chunks - 1 - t, nc - 1 - (t - ctx_chunks))

    def specs(order):
        return [pl.BlockSpec((L, GROUP_WIDTH), lambda b, g, t: (b * nc + order(t), g)),
                pl.BlockSpec((L, SSM_STATE), lambda b, g, t: (b * nc + order(t), b_col + g)),
                pl.BlockSpec((L, SSM_STATE), lambda b, g, t: (b * nc + order(t), c_col + g)),
                pl.BlockSpec((L, LANES), lambda b, g, t: (b * nc + order(t), g))]

    small = pl.BlockSpec((1, LANES), lambda b, g, t: (0, g))
    y_shape = jax.ShapeDtypeStruct((lay.m, D_MODEL), F32)
    return pl.pallas_call(
        _ssd_kernel,
        out_shape=(y_shape, y_shape),
        grid=(lay.batch, SSM_GROUPS, nc),
        in_specs=specs(fwd) + specs(rev) + [small, small],
        out_specs=(pl.BlockSpec((L, GROUP_WIDTH), lambda b, g, t: (b * nc + fwd(t), g)),
                   pl.BlockSpec((L, GROUP_WIDTH), lambda b, g, t: (b * nc + rev(t), g))),
        scratch_shapes=[pltpu.VMEM((SSM_STATE, GROUP_WIDTH), F32), pltpu.VMEM((SSM_STATE, GROUP_WIDTH), F32)],
        compiler_params=_params(("parallel", "parallel", "arbitrary"), 8 * L * GROUP_WIDTH * 4),
        name="ssd_scan",
    )(xbc, xbc, xbc, dt_raw, xbc, xbc, xbc, dt_raw, dt_bias_g, a_log_g)


def _ssd_out_kernel(yf_ref, yr_ref, x_ref, z_ref, d_ref, g_ref, o_ref):
    y = d_ref[...] * x_ref[...] + yf_ref[...] + yr_ref[...]
    yz = y * _silu(z_ref[...])
    for g in range(SSM_GROUPS):
        cols = slice(g * GROUP_WIDTH, (g + 1) * GROUP_WIDTH)
        v = yz[:, cols]
        inv = lax.rsqrt(jnp.mean(v * v, axis=-1, keepdims=True) + RMS_EPS)
        o_ref[:, cols] = (v * inv * g_ref[:, cols]).astype(o_ref.dtype)


def _ssd_out(lay, y_f, y_r, xbc, zxbc, d_exp, norm_g):
    row = pl.BlockSpec((TILE, D_MODEL), lambda i: (i, 0))
    vec = pl.BlockSpec((1, D_MODEL), lambda i: (0, 0))
    return pl.pallas_call(
        _ssd_out_kernel,
        out_shape=jax.ShapeDtypeStruct((lay.m, D_MODEL), BF16),
        grid=(lay.n_tiles,),
        in_specs=[row, row, row, row, vec, vec],
        out_specs=row,
        compiler_params=_params(("parallel",), 6 * TILE * D_MODEL * 4),
        name="ssd_gated_norm",
    )(y_f, y_r, xbc, zxbc, d_exp, norm_g)


def _rope_tables(seq):
    half = ATT_HEAD_DIM // 2
    nf = half // 2
    inv = ROPE_BASE ** (-(jnp.arange(nf, dtype=F32) / nf))
    rows = seq // GRID_W
    row = jnp.repeat(jnp.arange(rows, dtype=F32), GRID_W)
    col = jnp.tile(jnp.arange(GRID_W, dtype=F32), rows)
    lane = jnp.arange(2 * ATT_HEAD_DIM)
    freq = inv[lane % nf]
    pos = jnp.where((lane % ATT_HEAD_DIM) < half, row[:, None], col[:, None])
    ang = pos * freq[None, :]
    low = (lane % half) < nf
    sin = jnp.sin(ang)
    return jnp.cos(ang), jnp.where(low, -sin, 0.0), jnp.where(low, 0.0, sin)


def _rope(t, cos, sin_lo, sin_hi):
    nf = ATT_HEAD_DIM // 4
    return t * cos + pltpu.roll(t, LANES - nf, axis=1) * sin_lo + pltpu.roll(t, nf, axis=1) * sin_hi


def _dot_nt(a, b):
    return lax.dot_general(a, b, (((1,), (1,)), ((), ())), preferred_element_type=F32)


def _attn_kernel(q_ref, k_ref, v_ref, cos_ref, slo_ref, shi_ref, lam_ref, g_ref, o_ref,
                 krot_ref, kctx_ref, vb_ref, *, ctx, ctx_tiles, lam_init):
    qi = pl.program_id(2)
    heads = [slice(hh * LANES, (hh + 1) * LANES) for hh in range(ATT_HEAD_BLOCK)]

    @pl.when(qi == 0)
    def _():
        for cols in heads:
            krot_ref[:, cols] = _rope(k_ref[ctx:, cols], cos_ref[...], slo_ref[...], shi_ref[...]).astype(BF16)
        kctx_ref[...] = k_ref[:ctx, :].astype(BF16)
        vb_ref[...] = v_ref[...].astype(BF16)

    lp = lam_ref[...]
    lam = (jnp.exp(jnp.sum(lp[0:1] * lp[1:2], axis=-1, keepdims=True))
           - jnp.exp(jnp.sum(lp[2:3] * lp[3:4], axis=-1, keepdims=True)) + lam_init)
    first_map = lax.broadcasted_iota(jnp.int32, (TILE, LANES), 1) < ATT_HEAD_DIM

    def split(q):
        return jnp.where(first_map, q, 0.0).astype(BF16), jnp.where(first_map, 0.0, q).astype(BF16)

    def softmax_pv(queries, keys, values):
        scores = [_dot_nt(q, k) for q, k in zip(queries, keys)]
        mx = functools.reduce(jnp.maximum, [jnp.max(s, axis=-1, keepdims=True) for s in scores])
        e = [jnp.exp2(s - mx) for s in scores]
        total = sum(jnp.sum(x, axis=-1, keepdims=True) for x in e)
        pv = sum(jnp.dot(x.astype(BF16), v, preferred_element_type=F32) for x, v in zip(e, values))
        return pv * (1.0 / total)

    def attend(cols, queries, keys, values):
        o = softmax_pv(queries[0], keys, values) - lam * softmax_pv(queries[1], keys, values)
        inv = lax.rsqrt(jnp.mean(o * o, axis=-1, keepdims=True) + RMS_EPS)
        o_ref[:, cols] = (o * inv * g_ref[...] * (1.0 - lam_init)).astype(o_ref.dtype)

    @pl.when(qi < ctx_tiles)
    def _():
        for cols in heads:
            qn = split(q_ref[:, cols] * ATT_Q_SCALE)
            attend(cols, [[qn[0]], [qn[1]]], [kctx_ref[:, cols]], [vb_ref[:ctx, cols]])

    @pl.when(qi >= ctx_tiles)
    def _():
        start = pl.multiple_of((qi - ctx_tiles) * TILE, TILE)
        rows = pl.ds(start, TILE)
        for cols in heads:
            q = q_ref[:, cols] * ATT_Q_SCALE
            qn = split(q)
            qr = split(_rope(q, cos_ref[rows, :], slo_ref[rows, :], shi_ref[rows, :]))
            attend(cols, [[qn[0], qr[0]], [qn[1], qr[1]]], [kctx_ref[:, cols], krot_ref[:, cols]],
                   [vb_ref[:ctx, cols], vb_ref[ctx:, cols]])


def _attention(lay, qkv, tables, lam_p, subln_g, lam_init):
    width = ATT_HEAD_BLOCK * LANES
    hb = D_MODEL // width
    q0 = PROJ_Q // width
    tab = pl.BlockSpec((lay.seq, LANES), lambda b, h, i: (0, 0))
    kv_bytes = lay.nt * width * 4
    return pl.pallas_call(
        functools.partial(_attn_kernel, ctx=lay.ctx, ctx_tiles=lay.ctx_tiles, lam_init=lam_init),
        out_shape=jax.ShapeDtypeStruct((lay.m, D_MODEL), BF16),
        grid=(lay.batch, ATT_HEADS // ATT_HEAD_BLOCK, lay.tiles_per_batch),
        in_specs=[pl.BlockSpec((TILE, width), lambda b, h, i: (b * lay.tiles_per_batch + i, q0 + h)),
                  pl.BlockSpec((lay.nt, width), lambda b, h, i: (b, q0 + hb + h)),
                  pl.BlockSpec((lay.nt, width), lambda b, h, i: (b, q0 + 2 * hb + h)),
                  tab, tab, tab,
                  pl.BlockSpec((8, ATT_HEAD_DIM), lambda b, h, i: (0, 0)),
                  pl.BlockSpec((1, LANES), lambda b, h, i: (0, 0))],
        out_specs=pl.BlockSpec((TILE, width), lambda b, h, i: (b * lay.tiles_per_batch + i, h)),
        scratch_shapes=[pltpu.VMEM((lay.seq, width), BF16), pltpu.VMEM((lay.ctx, width), BF16),
                        pltpu.VMEM((lay.nt, width), BF16)],
        compiler_params=_params(("parallel", "parallel", "arbitrary"),
                                2 * kv_bytes + 3 * lay.seq * LANES * 4 + 4 * TILE * lay.nt * 4),
        name="diff_attention",
    )(qkv, qkv, qkv, *tables, lam_p, subln_g)


def _merge_kernel(ys_ref, ya_ref, ws_ref, wa_ref, gs_ref, ga_ref, o_ref):
    ps = jnp.dot(ys_ref[...], ws_ref[...], preferred_element_type=F32)
    pa = jnp.dot(ya_ref[...], wa_ref[...], preferred_element_type=F32)
    o_ref[...] = (jax.nn.sigmoid(gs_ref[...]) * ps + jax.nn.sigmoid(ga_ref[...]) * pa).astype(o_ref.dtype)


def _merge(lay, y_s, y_a, w_s, w_a, gates):
    tm, tn = 512, 1024
    goff = D_MODEL // tn
    g0 = PROJ_GATES // tn
    a_spec = pl.BlockSpec((tm, D_MODEL), lambda i, j: (i, 0))
    w_spec = pl.BlockSpec((D_MODEL, tn), lambda i, j: (0, j))
    return pl.pallas_call(
        _merge_kernel,
        out_shape=jax.ShapeDtypeStruct((lay.m, D_MODEL), BF16),
        grid=(lay.m // tm, D_MODEL // tn),
        in_specs=[a_spec, a_spec, w_spec, w_spec,
                  pl.BlockSpec((tm, tn), lambda i, j: (i, g0 + j)),
                  pl.BlockSpec((tm, tn), lambda i, j: (i, g0 + goff + j))],
        out_specs=pl.BlockSpec((tm, tn), lambda i, j: (i, j)),
        compiler_params=_params(("parallel", "parallel"),
                                2 * tm * D_MODEL * 2 + 2 * D_MODEL * tn * 2 + 3 * tm * tn * 4),
        name="branch_merge",
    )(y_s, y_a, w_s, w_a, gates, gates)


def _outproj_kernel(m_ref, w_ref, h_ref, gate_ref, lg_ref, lb_ref, sh_ref, sc_ref, h_out, u_out, *, alpha):
    o = jnp.dot(m_ref[...], w_ref[...], preferred_element_type=F32)
    h1 = _ln(alpha * h_ref[...] + gate_ref[...] * o) * lg_ref[...] + lb_ref[...]
    h_out[...] = h1
    u_out[...] = _ln(h1) * (1.0 + sc_ref[...]) + sh_ref[...]


def _outproj(lay, m, w_out, h, mod3, ln_g, ln_b, alpha):
    row = pl.BlockSpec((TILE, D_MODEL), lambda i: (i, 0))
    vec = pl.BlockSpec((1, D_MODEL), lambda i: (0, 0))
    shape = jax.ShapeDtypeStruct((lay.m, D_MODEL), F32)
    return pl.pallas_call(
        functools.partial(_outproj_kernel, alpha=alpha),
        out_shape=(shape, shape),
        grid=(lay.n_tiles,),
        in_specs=[row, pl.BlockSpec((D_MODEL, D_MODEL), lambda i: (0, 0)), row,
                  lay.mod_spec(2), vec, vec, lay.mod_spec(3), lay.mod_spec(4)],
        out_specs=(row, row),
        compiler_params=_params(("parallel",), D_MODEL * D_MODEL * 2 + 5 * TILE * D_MODEL * 4),
        name="out_proj_residual",
    )(m, w_out, h, mod3, ln_g, ln_b, mod3, mod3)


def _router_kernel(u_ref, w_ref, b_ref, ids_ref, wts_ref):
    logits = jnp.dot(u_ref[...], w_ref[...], precision=HIGHEST, preferred_element_type=F32) + b_ref[...]
    lane_i = lax.broadcasted_iota(jnp.int32, logits.shape, 1)
    lane = lane_i.astype(F32)
    none = float(LANES)
    neg = -jnp.inf

    def softmax(mask):
        v = jnp.where(mask, logits, neg)
        e = jnp.exp(v - jnp.max(v, axis=-1, keepdims=True))
        return e / jnp.sum(e, axis=-1, keepdims=True)

    def top(p, mask):
        best = jnp.max(jnp.where(mask, p, -1.0), axis=-1, keepdims=True)
        idx = jnp.min(jnp.where(mask & (p == best), lane, none), axis=-1, keepdims=True)
        return best, idx

    is_group = lane_i < MOE_GROUPS
    p_grp, grp = top(softmax(is_group), is_group)
    first = MOE_GROUPS + grp * EXPERTS_PER_GROUP
    in_grp = (lane >= first) & (lane < first + EXPERTS_PER_GROUP)
    pe = softmax(in_grp)
    p1, i1 = top(pe, in_grp)
    rest = in_grp & (lane != i1)
    p2, i2 = top(pe, rest)
    denom = p1 + p2
    w1 = p_grp * p1 / denom
    w2 = p_grp * p2 / denom
    ids = jnp.where(lane_i == 0, i1 - MOE_GROUPS, jnp.where(lane_i == 1, i2 - MOE_GROUPS, 0.0))
    ids_ref[...] = ids.astype(jnp.int32)
    wts_ref[...] = jnp.where(lane_i == 0, w1, jnp.where(lane_i == 1, w2, 0.0))


def _router(lay, u, w_r, b_r):
    row = pl.BlockSpec((TILE, D_MODEL), lambda i: (i, 0))
    out = pl.BlockSpec((TILE, LANES), lambda i: (i, 0))
    return pl.pallas_call(
        _router_kernel,
        out_shape=(jax.ShapeDtypeStruct((lay.m, LANES), jnp.int32), jax.ShapeDtypeStruct((lay.m, LANES), F32)),
        grid=(lay.n_tiles,),
        in_specs=[row, pl.BlockSpec((D_MODEL, LANES), lambda i: (0, 0)), pl.BlockSpec((1, LANES), lambda i: (0, 0))],
        out_specs=(out, out),
        compiler_params=_params(("parallel",), TILE * D_MODEL * 4 + D_MODEL * LANES * 4),
        name="moe_router",
    )(u, w_r, b_r)


def _dispatch(ids, n_tok):
    e_flat = ids[:, :MOE_TOP_K].reshape(-1)
    n_assign = n_tok * MOE_TOP_K
    onehot = (e_flat[:, None] == jnp.arange(N_EXPERTS, dtype=jnp.int32)[None, :]).astype(jnp.int32)
    csum = jnp.cumsum(onehot, axis=0)
    rank = jnp.sum(csum * onehot, axis=1) - 1
    counts = csum[-1]
    padded = (counts + MOE_BLOCK - 1) // MOE_BLOCK * MOE_BLOCK
    pend = jnp.cumsum(padded)
    pstarts = pend - padded
    dest = (pstarts[e_flat] + rank).astype(jnp.int32)
    n_blocks = n_assign // MOE_BLOCK + N_EXPERTS
    tok_flat = jnp.arange(n_assign, dtype=jnp.int32) // MOE_TOP_K
    slot_tok = jnp.zeros((n_blocks * MOE_BLOCK,), jnp.int32).at[dest].set(tok_flat)
    block_start = jnp.arange(n_blocks, dtype=jnp.int32) * MOE_BLOCK
    block_exp = jnp.minimum(jnp.sum(block_start[:, None] >= pend[None, :], axis=1), N_EXPERTS - 1).astype(jnp.int32)
    n_used = (pend[-1] // MOE_BLOCK).astype(jnp.int32).reshape(1)
    return dest, slot_tok, block_exp, n_used, n_blocks


def _row_copy(src_hbm, src_row, dst, dst_row, sem):
    return pltpu.make_async_copy(src_hbm.at[pl.ds(src_row, 1)], dst.at[pl.ds(dst_row, 1)], sem)


def _gather_rows(src_hbm, idx_ref, base, dst, sem, n_rows, start, dst_row=lambda r: r):
    for r in range(n_rows):
        cp = _row_copy(src_hbm, idx_ref[base + r], dst, dst_row(r), sem)
        if start:
            cp.start()
        else:
            cp.wait()


def _expert_kernel(bexp_ref, stok_ref, nused_ref, u_hbm, wg_ref, wu_ref, wd_ref, o_ref, xbuf, sem):
    i = pl.program_id(0)
    n = nused_ref[0]
    slot = i % 2

    def gather(blk, s, start):
        _gather_rows(u_hbm, stok_ref, blk * MOE_BLOCK, xbuf.at[s], sem.at[s], MOE_BLOCK, start)

    @pl.when(i == 0)
    def _():
        gather(0, 0, True)

    @pl.when(i < n)
    def _():
        gather(i, slot, False)
        gather(i + 1, 1 - slot, True)
        x = xbuf[slot].astype(BF16)
        hid = _silu(jnp.dot(x, wg_ref[...], preferred_element_type=F32)) * jnp.dot(x, wu_ref[...], preferred_element_type=F32)
        o_ref[...] = jnp.dot(hid.astype(BF16), wd_ref[...], preferred_element_type=F32)

    @pl.when(i == n - 1)
    def _():
        gather(i + 1, 1 - slot, False)

    @pl.when(i >= n)
    def _():
        o_ref[...] = jnp.zeros_like(o_ref)


def _experts(u, block_exp, slot_tok, n_used, n_blocks, w_gate, w_up, w_down):
    wspec_in = pl.BlockSpec((None, D_MODEL, EXPERT_FF), lambda i, be, st, nu: (be[i], 0, 0))
    grid_spec = pltpu.PrefetchScalarGridSpec(
        num_scalar_prefetch=3,
        grid=(n_blocks,),
        in_specs=[pl.BlockSpec(memory_space=pl.ANY), wspec_in, wspec_in,
                  pl.BlockSpec((None, EXPERT_FF, D_MODEL), lambda i, be, st, nu: (be[i], 0, 0))],
        out_specs=pl.BlockSpec((MOE_BLOCK, D_MODEL), lambda i, be, st, nu: (i, 0)),
        scratch_shapes=[pltpu.VMEM((2, MOE_BLOCK, D_MODEL), F32), pltpu.SemaphoreType.DMA((2,))],
    )
    return pl.pallas_call(
        _expert_kernel,
        out_shape=jax.ShapeDtypeStruct((n_blocks * MOE_BLOCK, D_MODEL), F32),
        grid_spec=grid_spec,
        compiler_params=_params(("arbitrary",), 3 * D_MODEL * EXPERT_FF * 2 + 2 * MOE_BLOCK * D_MODEL * 4),
        name="moe_experts",
    )(block_exp, slot_tok, n_used, u, w_gate, w_up, w_down)


def _combine_kernel(dest_ref, y_hbm, wts_ref, h_ref, gate_ref, lg_ref, lb_ref, sh_ref, sc_ref, h_out, u_out,
                    ybuf, sem, *, alpha, n_tiles):
    i = pl.program_id(0)
    slot = i % 2
    rows = TILE * MOE_TOP_K

    def gather(tile, s, start):
        _gather_rows(y_hbm, dest_ref, tile * rows, ybuf.at[s], sem.at[s], rows, start,
                     dst_row=lambda r: (r % MOE_TOP_K) * TILE + r // MOE_TOP_K)

    @pl.when(i == 0)
    def _():
        gather(0, 0, True)

    ahead = jnp.minimum(i + 1, n_tiles - 1)
    gather(i, slot, False)
    gather(ahead, 1 - slot, True)
    wts = wts_ref[...]
    y = wts[:, 0:1] * ybuf[slot, 0:TILE, :] + wts[:, 1:2] * ybuf[slot, TILE:2 * TILE, :]
    h2 = _ln(alpha * h_ref[...] + gate_ref[...] * y) * lg_ref[...] + lb_ref[...]
    h_out[...] = h2
    u_out[...] = (_ln(h2) * (1.0 + sc_ref[...]) + sh_ref[...]).astype(u_out.dtype)

    @pl.when(i == n_tiles - 1)
    def _():
        gather(ahead, 1 - slot, False)


def _combine(lay, dest, y_slots, wts, h, mod3, mod3_next, ln_g, ln_b, alpha):
    row = pl.BlockSpec((TILE, D_MODEL), lambda i, d: (i, 0))
    vec = pl.BlockSpec((1, D_MODEL), lambda i, d: (0, 0))
    grid_spec = pltpu.PrefetchScalarGridSpec(
        num_scalar_prefetch=1,
        grid=(lay.n_tiles,),
        in_specs=[pl.BlockSpec(memory_space=pl.ANY), pl.BlockSpec((TILE, LANES), lambda i, d: (i, 0)), row,
                  lay.mod_spec(5), vec, vec, lay.mod_spec(0), lay.mod_spec(1)],
        out_specs=(row, row),
        scratch_shapes=[pltpu.VMEM((2, MOE_TOP_K * TILE, D_MODEL), F32), pltpu.SemaphoreType.DMA((2,))],
    )
    return pl.pallas_call(
        functools.partial(_combine_kernel, alpha=alpha, n_tiles=lay.n_tiles),
        out_shape=(jax.ShapeDtypeStruct((lay.m, D_MODEL), F32), jax.ShapeDtypeStruct((lay.m, D_MODEL), BF16)),
        grid_spec=grid_spec,
        compiler_params=_params(("arbitrary",), 5 * TILE * D_MODEL * 4),
        name="moe_combine_residual",
    )(dest, y_slots, wts, h, mod3, ln_g, ln_b, mod3_next, mod3_next)


def _group_lanes(v):
    per_group = v.reshape(2, SSM_GROUPS, HEADS_PER_GROUP).transpose(1, 0, 2).reshape(SSM_GROUPS, 2 * HEADS_PER_GROUP)
    return jnp.pad(per_group, ((0, 0), (0, LANES - 2 * HEADS_PER_GROUP))).reshape(1, SSM_GROUPS * LANES)


def kernel(x, c, ctx, c_ctx, w_ada, b_ada, w_in, conv_w, conv_b, ssm_dt_bias, ssm_a_log, ssm_d, ssm_norm_g, lam_q1, lam_k1, lam_q2, lam_k2, attn_subln_g, w_br_ssm, w_br_att, w_out, ln1_g, ln1_b, w_router_group, b_router_group, w_router_expert, b_router_expert, w_exp_gate, w_exp_up, w_exp_down, ln2_g, ln2_b):
    batch, seq, d = x.shape
    depth = w_ada.shape[0]
    assert d == D_MODEL and batch < MOD_ROWS
    lay = _Layout(batch, seq, ctx.shape[1])
    alpha = (2.0 * depth) ** 0.25

    c_all = jnp.zeros((MOD_ROWS, d), F32).at[:batch].set(c).at[batch].set(c_ctx)
    mod = _adaln(c_all, w_ada, b_ada).reshape(depth, MOD_ROWS, 1, 6 * d)
    tables = _rope_tables(seq)

    w_proj = _pack_w_in(w_in)
    w_dt = w_in[:, :, IN_DT:IN_Q].reshape(depth, d, 2, SSM_GROUPS, HEADS_PER_GROUP).transpose(0, 1, 3, 2, 4)
    w_dt = jnp.pad(w_dt.reshape(depth, d, SSM_GROUPS, 2 * HEADS_PER_GROUP),
                   ((0, 0), (0, 0), (0, 0), (0, LANES - 2 * HEADS_PER_GROUP)))
    w_dt = w_dt.reshape(depth, d, SSM_GROUPS * LANES).astype(BF16)

    h = jnp.concatenate([ctx, x], axis=1).reshape(lay.m, d)
    u = _lnmod(lay, h, mod[0], 0, 1, BF16)
    for l in range(depth):
        lam_init = 0.8 - 0.6 * math.exp(-0.3 * l)
        proj = _matmul(u, w_proj, l)
        dt_raw = _matmul(u, w_dt, l)

        xbc = _conv_silu(lay, proj, conv_w[l], conv_b[l])
        y_f, y_r = _ssd(lay, xbc, dt_raw, _group_lanes(ssm_dt_bias[l]), _group_lanes(ssm_a_log[l]))
        y_s = _ssd_out(lay, y_f, y_r, xbc, proj, jnp.repeat(ssm_d[l], SSM_HEAD_DIM).reshape(1, d),
                       ssm_norm_g[l].reshape(1, d))

        lam_p = jnp.zeros((8, ATT_HEAD_DIM), F32).at[0].set(lam_q1[l]).at[1].set(lam_k1[l]).at[2].set(lam_q2[l]).at[3].set(lam_k2[l])
        y_a = _attention(lay, proj, tables, lam_p, attn_subln_g[l].reshape(1, LANES), lam_init)

        m = _merge(lay, y_s, y_a, w_br_ssm[l].astype(BF16), w_br_att[l].astype(BF16), proj)
        h, u2 = _outproj(lay, m, w_out[l].astype(BF16), h, mod[l], ln1_g[l].reshape(1, d), ln1_b[l].reshape(1, d), alpha)

        w_r = jnp.zeros((d, LANES), F32).at[:, :MOE_GROUPS].set(w_router_group[l]).at[:, MOE_GROUPS:MOE_GROUPS + N_EXPERTS].set(w_router_expert[l])
        b_r = jnp.zeros((1, LANES), F32).at[0, :MOE_GROUPS].set(b_router_group[l]).at[0, MOE_GROUPS:MOE_GROUPS + N_EXPERTS].set(b_router_expert[l])
        ids, wts = _router(lay, u2, w_r, b_r)
        dest, slot_tok, block_exp, n_used, n_blocks = _dispatch(ids, lay.m)
        y_slots = _experts(u2, block_exp, slot_tok, n_used, n_blocks,
                           w_exp_gate[l].astype(BF16), w_exp_up[l].astype(BF16), w_exp_down[l].astype(BF16))
        h, u = _combine(lay, dest, y_slots, wts, h, mod[l], mod[min(l + 1, depth - 1)],
                        ln2_g[l].reshape(1, d), ln2_b[l].reshape(1, d), alpha)
    return h.reshape(batch, lay.nt, d)[:, lay.ctx:]
```

```python
import functools
import math

import jax
import jax.numpy as jnp
from jax import lax
from jax.experimental import pallas as pl
from jax.experimental.pallas import tpu as pltpu

F32 = jnp.float32
BF16 = jnp.bfloat16
HIGHEST = lax.Precision.HIGHEST

D_MODEL = 2048
GRID_W = 64

SSM_HEAD_DIM = 64
SSM_HEADS = D_MODEL // SSM_HEAD_DIM
SSM_GROUPS = 4
SSM_STATE = 128
HEADS_PER_GROUP = SSM_HEADS // SSM_GROUPS
GROUP_WIDTH = HEADS_PER_GROUP * SSM_HEAD_DIM
CONV_WIDTH = 5
CONV_DIM = D_MODEL + 2 * SSM_GROUPS * SSM_STATE
SSD_CHUNK = 128
SSD_GROUP_BLOCK = 2

ATT_HEAD_DIM = 64
ATT_HEADS = D_MODEL // (2 * ATT_HEAD_DIM)
ROPE_BASE = 10000.0
ATT_HEAD_BLOCK = 2
ATT_Q_SCALE = ATT_HEAD_DIM ** -0.5 * math.log2(math.e)

MOE_GROUPS = 4
EXPERTS_PER_GROUP = 8
N_EXPERTS = MOE_GROUPS * EXPERTS_PER_GROUP
MOE_TOP_K = 2
EXPERT_FF = D_MODEL // 2

LN_EPS = 1e-6
RMS_EPS = 1e-5

LANES = 128
MOD_ROWS = 16
TILE = 256
MOE_BLOCK = 256
VMEM_CAP = 60 * 1024 * 1024
VMEM_SLACK = 6 * 1024 * 1024


def _params(sem, block_bytes):
    return pltpu.CompilerParams(dimension_semantics=sem,
                                vmem_limit_bytes=int(min(VMEM_CAP, 2 * block_bytes + VMEM_SLACK)))


def _ln(x):
    mu = jnp.mean(x, axis=-1, keepdims=True)
    xc = x - mu
    var = jnp.mean(xc * xc, axis=-1, keepdims=True)
    return xc * lax.rsqrt(var + LN_EPS)


def _silu(x):
    return x * jax.nn.sigmoid(x)


def _mod_kernel(c_ref, w_ref, b_ref, o_ref):
    sc = _silu(c_ref[...]).astype(BF16)
    o_ref[...] = jnp.dot(sc, w_ref[...].astype(BF16), preferred_element_type=F32) + b_ref[...]


def _adaln(c_all, w_ada, b_ada):
    depth, d, n = w_ada.shape
    tn = 1024
    return pl.pallas_call(
        _mod_kernel,
        out_shape=jax.ShapeDtypeStruct((depth, MOD_ROWS, n), F32),
        grid=(depth, n // tn),
        in_specs=[pl.BlockSpec((MOD_ROWS, d), lambda l, j: (0, 0)),
                  pl.BlockSpec((None, d, tn), lambda l, j: (l, 0, j)),
                  pl.BlockSpec((None, 1, tn), lambda l, j: (l, 0, j))],
        out_specs=pl.BlockSpec((None, MOD_ROWS, tn), lambda l, j: (l, 0, j)),
        compiler_params=_params(("parallel", "parallel"), d * tn * 4 + d * tn * 2),
        name="adaln_mod",
    )(c_all, w_ada, b_ada.reshape(depth, 1, n))


class _Layout:
    def __init__(self, batch, seq, ctx):
        assert ctx % TILE == 0 and seq % TILE == 0 and seq % GRID_W == 0
        self.batch, self.seq, self.ctx = batch, seq, ctx
        self.nt = ctx + seq
        self.m = batch * self.nt
        self.tiles_per_batch = self.nt // TILE
        self.ctx_tiles = ctx // TILE
        self.n_tiles = self.m // TILE

    def mod_row(self, i):
        return jnp.where(i % self.tiles_per_batch < self.ctx_tiles, self.batch, i // self.tiles_per_batch)

    def mod_spec(self, chunk):
        return pl.BlockSpec((None, 1, D_MODEL), lambda i, *_: (self.mod_row(i), 0, chunk))

    def full_tile(self, i):
        return i

    def full_row_spec(self, col_block=0):
        return pl.BlockSpec((TILE, D_MODEL), lambda i, *_: (self.full_tile(i), col_block))


class _LatentLayout(_Layout):
    def __init__(self, full):
        self.batch, self.seq, self.ctx = full.batch, full.seq, full.ctx
        self.m = full.batch * full.seq
        self.tiles_per_batch = full.seq // TILE
        self.n_tiles = self.m // TILE
        self._full = full

    def mod_row(self, i):
        return i // self.tiles_per_batch

    def full_tile(self, i):
        f = self._full
        return (i // self.tiles_per_batch) * f.tiles_per_batch + f.ctx_tiles + i % self.tiles_per_batch


def _lnmod_kernel(h_ref, sh_ref, sc_ref, u_ref):
    u_ref[...] = (_ln(h_ref[...]) * (1.0 + sc_ref[...]) + sh_ref[...]).astype(u_ref.dtype)


def _lnmod(lay, h, mod3, shift_chunk, scale_chunk, out_dtype):
    row = pl.BlockSpec((TILE, D_MODEL), lambda i: (i, 0))
    return pl.pallas_call(
        _lnmod_kernel,
        out_shape=jax.ShapeDtypeStruct((lay.m, D_MODEL), out_dtype),
        grid=(lay.n_tiles,),
        in_specs=[row, lay.mod_spec(shift_chunk), lay.mod_spec(scale_chunk)],
        out_specs=row,
        compiler_params=_params(("parallel",), 2 * TILE * D_MODEL * 4),
        name="ln_modulate",
    )(h, mod3, mod3)


def _mm_kernel(a_ref, w_ref, o_ref):
    o_ref[...] = jnp.dot(a_ref[...], w_ref[...], preferred_element_type=F32).astype(o_ref.dtype)


def _matmul(a, w, layer, out_dtype=F32, tm=1024, tn=1024):
    m, k = a.shape
    n = w.shape[2]
    tn = min(tn, n)
    tm = math.gcd(m, tm)
    assert tm % TILE == 0 and n % tn == 0
    return pl.pallas_call(
        _mm_kernel,
        out_shape=jax.ShapeDtypeStruct((m, n), out_dtype),
        grid=(m // tm, n // tn),
        in_specs=[pl.BlockSpec((tm, k), lambda i, j: (i, 0)),
                  pl.BlockSpec((None, k, tn), lambda i, j: (layer, 0, j))],
        out_specs=pl.BlockSpec((tm, tn), lambda i, j: (i, j)),
        compiler_params=_params(("parallel", "parallel"), tm * k * 2 + k * tn * 2 + tm * tn * 4),
        name="matmul",
    )(a, w)


IN_XBC = D_MODEL
IN_DT = IN_XBC + CONV_DIM
IN_Q = IN_DT + 2 * SSM_HEADS
IN_GATES = IN_Q + 3 * D_MODEL
PROJ_Z = 0
PROJ_GATES = PROJ_Z + D_MODEL
PROJ_Q = PROJ_GATES + 2 * D_MODEL
PROJ_XBC = PROJ_Q + 3 * D_MODEL
PROJ_COLS = PROJ_XBC + CONV_DIM


def _regroup_w_in(w_in):
    return jnp.concatenate([w_in[:, :, :IN_XBC], w_in[:, :, IN_GATES:], w_in[:, :, IN_Q:IN_GATES],
                            w_in[:, :, IN_XBC:IN_DT]], axis=-1).astype(BF16)


def _conv_kernel(x_ref, w_ref, b_ref, o_ref, *, ctx):
    x = x_ref[...]
    nt = x.shape[0]
    t = lax.broadcasted_iota(jnp.int32, x.shape, 0)
    lo = jnp.where(t < ctx, 0, ctx)
    hi = jnp.where(t < ctx, ctx, nt)
    pad = (CONV_WIDTH - 1) // 2
    acc = b_ref[...] + x * w_ref[pad:pad + 1, :]
    for k in range(CONV_WIDTH):
        d = k - pad
        if d == 0:
            continue
        shifted = pltpu.roll(x, (-d) % nt, axis=0)
        valid = (t + d >= lo) & (t + d < hi)
        acc = acc + jnp.where(valid, shifted, 0.0) * w_ref[k:k + 1, :]
    o_ref[...] = _silu(acc)


def _conv_silu(lay, proj, conv_w, conv_b):
    tc = 512
    off = PROJ_XBC // tc
    return pl.pallas_call(
        functools.partial(_conv_kernel, ctx=lay.ctx),
        out_shape=jax.ShapeDtypeStruct((lay.m, CONV_DIM), F32),
        grid=(lay.batch, CONV_DIM // tc),
        in_specs=[pl.BlockSpec((lay.nt, tc), lambda b, j: (b, off + j)),
                  pl.BlockSpec((CONV_WIDTH, tc), lambda b, j: (0, j)),
                  pl.BlockSpec((1, tc), lambda b, j: (0, j))],
        out_specs=pl.BlockSpec((lay.nt, tc), lambda b, j: (b, j)),
        compiler_params=_params(("parallel", "parallel"), 6 * lay.nt * tc * 4),
        name="conv_silu",
    )(proj, conv_w, conv_b.reshape(1, CONV_DIM))


def _ssd_direction(x_ref, b_ref, c_ref, dt_ref, bias_ref, alog_ref, st_ref, y_ref, *, lane_off, upper):
    L = SSD_CHUNK
    r = lax.broadcasted_iota(jnp.int32, (L, L), 0)
    s = lax.broadcasted_iota(jnp.int32, (L, L), 1)
    tri = (s >= r) if upper else (s <= r)
    x = x_ref[...]
    bm = b_ref[...]
    cm = c_ref[...].astype(BF16)
    dtv = dt_ref[...] + bias_ref[...]
    dt = jnp.maximum(dtv, 0.0) + jnp.log(1.0 + jnp.exp(-jnp.abs(dtv)))
    adt = dt * (-jnp.exp(alog_ref[...]))
    cs = jnp.dot(tri.astype(F32), adt, precision=HIGHEST, preferred_element_type=F32)
    cs_t = cs.T
    a_tot = jnp.sum(adt, axis=0, keepdims=True)
    cb = lax.dot_general(cm, bm.astype(BF16), (((1,), (1,)), ((), ())), preferred_element_type=F32)
    y_off = jnp.dot(cm, st_ref[...].astype(BF16), preferred_element_type=F32)
    bm_t = bm.T.astype(BF16)
    first_half = lax.broadcasted_iota(jnp.int32, (L, LANES), 1) < SSM_HEAD_DIM
    first_half_row = first_half[0:1, :]
    for pair in range(HEADS_PER_GROUP // 2):
        la = lane_off + 2 * pair
        lb = la + 1
        cols = slice(pair * LANES, (pair + 1) * LANES)
        cs_pair = jnp.where(first_half, cs[:, la:la + 1], cs[:, lb:lb + 1])
        dt_pair = jnp.where(first_half, dt[:, la:la + 1], dt[:, lb:lb + 1])
        tot_pair = jnp.where(first_half_row, a_tot[:, la:la + 1], a_tot[:, lb:lb + 1])
        xdt = x[:, cols] * dt_pair
        y_pair = y_off[:, cols] * jnp.exp(cs_pair)
        for lane, keep in ((la, first_half), (lb, jnp.logical_not(first_half))):
            seg = cs[:, lane:lane + 1] - cs_t[lane:lane + 1, :]
            decay = jnp.where(tri, jnp.exp(jnp.where(tri, seg, 0.0)), 0.0)
            mh = (cb * decay).astype(BF16)
            y_pair = y_pair + jnp.dot(mh, jnp.where(keep, xdt, 0.0).astype(BF16), preferred_element_type=F32)
        y_ref[:, cols] = y_pair
        z = (xdt * jnp.exp(tot_pair - cs_pair)).astype(BF16)
        st_ref[:, cols] = st_ref[:, cols] * jnp.exp(tot_pair) + jnp.dot(bm_t, z, preferred_element_type=F32)


def _ssd_kernel(xf, bf, cf, dtf, xr, br, cr, dtr, bias_ref, alog_ref, yf_ref, yr_ref, stf_ref, str_ref):
    @pl.when(pl.program_id(2) == 0)
    def _():
        stf_ref[...] = jnp.zeros_like(stf_ref)
        str_ref[...] = jnp.zeros_like(str_ref)

    for g in range(SSD_GROUP_BLOCK):
        wide = (slice(None), slice(g * GROUP_WIDTH, (g + 1) * GROUP_WIDTH))
        lanes = (slice(None), slice(g * LANES, (g + 1) * LANES))
        small = (bias_ref.at[lanes], alog_ref.at[lanes])
        _ssd_direction(xf.at[wide], bf.at[lanes], cf.at[lanes], dtf.at[lanes], *small, stf_ref.at[wide], yf_ref.at[wide],
                       lane_off=0, upper=False)
        _ssd_direction(xr.at[wide], br.at[lanes], cr.at[lanes], dtr.at[lanes], *small, str_ref.at[wide], yr_ref.at[wide],
                       lane_off=HEADS_PER_GROUP, upper=True)


def _ssd(lay, xbc, dt_raw, dt_bias_g, a_log_g):
    L = SSD_CHUNK
    nc = lay.nt // L
    ctx_chunks = lay.ctx // L
    gb = SSD_GROUP_BLOCK
    b_col = D_MODEL // (gb * SSM_STATE)
    c_col = b_col + SSM_GROUPS // gb

    def fwd(t):
        return t

    def rev(t):
        return jnp.where(t < ctx_chunks, ctx_chunks - 1 - t, nc - 1 - (t - ctx_chunks))

    def specs(order):
        return [pl.BlockSpec((L, gb * GROUP_WIDTH), lambda b, g, t: (b * nc + order(t), g)),
                pl.BlockSpec((L, gb * SSM_STATE), lambda b, g, t: (b * nc + order(t), b_col + g)),
                pl.BlockSpec((L, gb * SSM_STATE), lambda b, g, t: (b * nc + order(t), c_col + g)),
                pl.BlockSpec((L, gb * LANES), lambda b, g, t: (b * nc + order(t), g))]

    small = pl.BlockSpec((1, gb * LANES), lambda b, g, t: (0, g))
    y_shape = jax.ShapeDtypeStruct((lay.m, D_MODEL), F32)
    state = pltpu.VMEM((SSM_STATE, gb * GROUP_WIDTH), F32)
    return pl.pallas_call(
        _ssd_kernel,
        out_shape=(y_shape, y_shape),
        grid=(lay.batch, SSM_GROUPS // gb, nc),
        in_specs=specs(fwd) + specs(rev) + [small, small],
        out_specs=(pl.BlockSpec((L, gb * GROUP_WIDTH), lambda b, g, t: (b * nc + fwd(t), g)),
                   pl.BlockSpec((L, gb * GROUP_WIDTH), lambda b, g, t: (b * nc + rev(t), g))),
        scratch_shapes=[state, state],
        compiler_params=_params(("parallel", "parallel", "arbitrary"), 8 * L * gb * GROUP_WIDTH * 4),
        name="ssd_scan",
    )(xbc, xbc, xbc, dt_raw, xbc, xbc, xbc, dt_raw, dt_bias_g, a_log_g)


def _ssd_out_kernel(yf_ref, yr_ref, x_ref, z_ref, d_ref, g_ref, o_ref):
    y = d_ref[...] * x_ref[...] + yf_ref[...] + yr_ref[...]
    yz = y * _silu(z_ref[...])
    for g in range(SSM_GROUPS):
        cols = slice(g * GROUP_WIDTH, (g + 1) * GROUP_WIDTH)
        v = yz[:, cols]
        inv = lax.rsqrt(jnp.mean(v * v, axis=-1, keepdims=True) + RMS_EPS)
        o_ref[:, cols] = (v * inv * g_ref[:, cols]).astype(o_ref.dtype)


def _ssd_out(lay, y_f, y_r, xbc, proj, d_exp, norm_g):
    src = lay.full_row_spec()
    vec = pl.BlockSpec((1, D_MODEL), lambda i: (0, 0))
    return pl.pallas_call(
        _ssd_out_kernel,
        out_shape=jax.ShapeDtypeStruct((lay.m, D_MODEL), BF16),
        grid=(lay.n_tiles,),
        in_specs=[src, src, src, src, vec, vec],
        out_specs=pl.BlockSpec((TILE, D_MODEL), lambda i: (i, 0)),
        compiler_params=_params(("parallel",), 6 * TILE * D_MODEL * 4),
        name="ssd_gated_norm",
    )(y_f, y_r, xbc, proj, d_exp, norm_g)


def _rope_tables(seq):
    half = ATT_HEAD_DIM // 2
    nf = half // 2
    inv = ROPE_BASE ** (-(jnp.arange(nf, dtype=F32) / nf))
    rows = seq // GRID_W
    row = jnp.repeat(jnp.arange(rows, dtype=F32), GRID_W)
    col = jnp.tile(jnp.arange(GRID_W, dtype=F32), rows)
    lane = jnp.arange(2 * ATT_HEAD_DIM)
    freq = inv[lane % nf]
    pos = jnp.where((lane % ATT_HEAD_DIM) < half, row[:, None], col[:, None])
    ang = pos * freq[None, :]
    low = (lane % half) < nf
    sin = jnp.sin(ang)
    return jnp.cos(ang), jnp.where(low, -sin, 0.0), jnp.where(low, 0.0, sin)


def _rope(t, cos, sin_lo, sin_hi):
    nf = ATT_HEAD_DIM // 4
    return t * cos + pltpu.roll(t, LANES - nf, axis=1) * sin_lo + pltpu.roll(t, nf, axis=1) * sin_hi


def _dot_nt(a, b):
    return lax.dot_general(a, b, (((1,), (1,)), ((), ())), preferred_element_type=F32)


def _attn_kernel(q_ref, k_ref, v_ref, cos_ref, slo_ref, shi_ref, lam_ref, g_ref, o_ref,
                 krot_ref, kctx_ref, vb_ref, *, ctx, ctx_tiles, lam_init):
    qi = pl.program_id(2)
    heads = [slice(hh * LANES, (hh + 1) * LANES) for hh in range(ATT_HEAD_BLOCK)]

    @pl.when(qi == 0)
    def _():
        for cols in heads:
            krot_ref[:, cols] = _rope(k_ref[ctx:, cols], cos_ref[...], slo_ref[...], shi_ref[...]).astype(BF16)
        kctx_ref[...] = k_ref[:ctx, :].astype(BF16)
        vb_ref[...] = v_ref[...].astype(BF16)

    lp = lam_ref[...]
    lam = (jnp.exp(jnp.sum(lp[0:1] * lp[1:2], axis=-1, keepdims=True))
           - jnp.exp(jnp.sum(lp[2:3] * lp[3:4], axis=-1, keepdims=True)) + lam_init)
    first_map = lax.broadcasted_iota(jnp.int32, (TILE, LANES), 1) < ATT_HEAD_DIM

    def split(q):
        return jnp.where(first_map, q, 0.0).astype(BF16), jnp.where(first_map, 0.0, q).astype(BF16)

    def softmax_pv(queries, keys, values):
        scores = [_dot_nt(q, k) for q, k in zip(queries, keys)]
        mx = functools.reduce(jnp.maximum, [jnp.max(s, axis=-1, keepdims=True) for s in scores])
        e = [jnp.exp2(s - mx) for s in scores]
        total = sum(jnp.sum(x, axis=-1, keepdims=True) for x in e)
        pv = sum(jnp.dot(x.astype(BF16), v, preferred_element_type=F32) for x, v in zip(e, values))
        return pv * (1.0 / total)

    def attend(cols, queries, keys, values):
        o = softmax_pv(queries[0], keys, values) - lam * softmax_pv(queries[1], keys, values)
        inv = lax.rsqrt(jnp.mean(o * o, axis=-1, keepdims=True) + RMS_EPS)
        o_ref[:, cols] = (o * inv * g_ref[...] * (1.0 - lam_init)).astype(o_ref.dtype)

    @pl.when(qi < ctx_tiles)
    def _():
        for cols in heads:
            qn = split(q_ref[:, cols] * ATT_Q_SCALE)
            attend(cols, [[qn[0]], [qn[1]]], [kctx_ref[:, cols]], [vb_ref[:ctx, cols]])

    @pl.when(qi >= ctx_tiles)
    def _():
        start = pl.multiple_of((qi - ctx_tiles) * TILE, TILE)
        rows = pl.ds(start, TILE)
        for cols in heads:
            q = q_ref[:, cols] * ATT_Q_SCALE
            qn = split(q)
            qr = split(_rope(q, cos_ref[rows, :], slo_ref[rows, :], shi_ref[rows, :]))
            attend(cols, [[qn[0], qr[0]], [qn[1], qr[1]]], [kctx_ref[:, cols], krot_ref[:, cols]],
                   [vb_ref[:ctx, cols], vb_ref[ctx:, cols]])


def _attention(lay, qkv, tables, lam_p, subln_g, lam_init):
    width = ATT_HEAD_BLOCK * LANES
    hb = D_MODEL // width
    q0 = PROJ_Q // width
    tab = pl.BlockSpec((lay.seq, LANES), lambda b, h, i: (0, 0))
    kv_bytes = lay.nt * width * 4
    return pl.pallas_call(
        functools.partial(_attn_kernel, ctx=lay.ctx, ctx_tiles=lay.ctx_tiles, lam_init=lam_init),
        out_shape=jax.ShapeDtypeStruct((lay.m, D_MODEL), BF16),
        grid=(lay.batch, ATT_HEADS // ATT_HEAD_BLOCK, lay.tiles_per_batch),
        in_specs=[pl.BlockSpec((TILE, width), lambda b, h, i: (b * lay.tiles_per_batch + i, q0 + h)),
                  pl.BlockSpec((lay.nt, width), lambda b, h, i: (b, q0 + hb + h)),
                  pl.BlockSpec((lay.nt, width), lambda b, h, i: (b, q0 + 2 * hb + h)),
                  tab, tab, tab,
                  pl.BlockSpec((8, ATT_HEAD_DIM), lambda b, h, i: (0, 0)),
                  pl.BlockSpec((1, LANES), lambda b, h, i: (0, 0))],
        out_specs=pl.BlockSpec((TILE, width), lambda b, h, i: (b * lay.tiles_per_batch + i, h)),
        scratch_shapes=[pltpu.VMEM((lay.seq, width), BF16), pltpu.VMEM((lay.ctx, width), BF16),
                        pltpu.VMEM((lay.nt, width), BF16)],
        compiler_params=_params(("parallel", "parallel", "arbitrary"),
                                2 * kv_bytes + 3 * lay.seq * LANES * 4 + 4 * TILE * lay.nt * 4),
        name="diff_attention",
    )(qkv, qkv, qkv, *tables, lam_p, subln_g)


def _route(logits):
    lane_i = lax.broadcasted_iota(jnp.int32, logits.shape, 1)
    lane = lane_i.astype(F32)
    none = float(LANES)
    neg = -jnp.inf

    def softmax(mask):
        v = jnp.where(mask, logits, neg)
        e = jnp.exp(v - jnp.max(v, axis=-1, keepdims=True))
        return e / jnp.sum(e, axis=-1, keepdims=True)

    def top(p, mask):
        best = jnp.max(jnp.where(mask, p, -1.0), axis=-1, keepdims=True)
        idx = jnp.min(jnp.where(mask & (p == best), lane, none), axis=-1, keepdims=True)
        return best, idx

    is_group = lane_i < MOE_GROUPS
    p_grp, grp = top(softmax(is_group), is_group)
    first = MOE_GROUPS + grp * EXPERTS_PER_GROUP
    in_grp = (lane >= first) & (lane < first + EXPERTS_PER_GROUP)
    pe = softmax(in_grp)
    p1, i1 = top(pe, in_grp)
    rest = in_grp & (lane != i1)
    p2, i2 = top(pe, rest)
    denom = p1 + p2
    w1 = p_grp * p1 / denom
    w2 = p_grp * p2 / denom
    ids = jnp.where(lane_i == 0, i1 - MOE_GROUPS, jnp.where(lane_i == 1, i2 - MOE_GROUPS, 0.0))
    return ids, jnp.where(lane_i == 0, w1, jnp.where(lane_i == 1, w2, 0.0))


def _split_bf16(x):
    hi = x.astype(BF16)
    return hi, (x - hi.astype(F32)).astype(BF16)


def _mixer_out_kernel(ys_ref, ya_ref, ws_ref, wa_ref, wo_ref, gs_ref, ga_ref, h_ref, gate_ref, lg_ref, lb_ref,
                      sh_ref, sc_ref, rh_ref, rl_ref, rb_ref, h_out, u_out, ids_out, wts_out, *, alpha):
    ps = jnp.dot(ys_ref[...], ws_ref[...], preferred_element_type=F32)
    pa = jnp.dot(ya_ref[...], wa_ref[...], preferred_element_type=F32)
    m = (jax.nn.sigmoid(gs_ref[...]) * ps + jax.nn.sigmoid(ga_ref[...]) * pa).astype(BF16)
    o = jnp.dot(m, wo_ref[...], preferred_element_type=F32)
    h1 = _ln(alpha * h_ref[...] + gate_ref[...] * o) * lg_ref[...] + lb_ref[...]
    h_out[...] = h1
    u = _ln(h1) * (1.0 + sc_ref[...]) + sh_ref[...]
    u_out[...] = u
    u_hi, u_lo = _split_bf16(u)
    logits = (jnp.dot(u_hi, rh_ref[...], preferred_element_type=F32)
              + (jnp.dot(u_hi, rl_ref[...], preferred_element_type=F32)
                 + jnp.dot(u_lo, rh_ref[...], preferred_element_type=F32))) + rb_ref[...]
    ids, wts = _route(logits)
    ids_out[...] = ids.astype(jnp.int32)
    wts_out[...] = wts


def _mixer_out(lay, layer, y_s, y_a, w_s, w_a, w_o, proj, h, mod3, ln_g, ln_b, r_hi, r_lo, r_b, alpha):
    row = pl.BlockSpec((TILE, D_MODEL), lambda i: (i, 0))
    vec = pl.BlockSpec((1, D_MODEL), lambda i: (0, 0))
    weight = pl.BlockSpec((None, D_MODEL, D_MODEL), lambda i: (layer, 0, 0), pipeline_mode=pl.Buffered(1))
    rw = pl.BlockSpec((D_MODEL, LANES), lambda i: (0, 0))
    small = pl.BlockSpec((TILE, LANES), lambda i: (i, 0))
    g0 = PROJ_GATES // D_MODEL
    shape = jax.ShapeDtypeStruct((lay.m, D_MODEL), F32)
    return pl.pallas_call(
        functools.partial(_mixer_out_kernel, alpha=alpha),
        out_shape=(shape, shape, jax.ShapeDtypeStruct((lay.m, LANES), jnp.int32),
                   jax.ShapeDtypeStruct((lay.m, LANES), F32)),
        grid=(lay.n_tiles,),
        in_specs=[row, lay.full_row_spec(), weight, weight, weight,
                  lay.full_row_spec(g0), lay.full_row_spec(g0 + 1),
                  lay.full_row_spec(), lay.mod_spec(2), vec, vec, lay.mod_spec(3), lay.mod_spec(4),
                  rw, rw, pl.BlockSpec((1, LANES), lambda i: (0, 0))],
        out_specs=(row, row, small, small),
        compiler_params=pltpu.CompilerParams(
            dimension_semantics=("parallel",),
            vmem_limit_bytes=int(min(VMEM_CAP, 3 * D_MODEL * D_MODEL * 2 + 2 * 7 * TILE * D_MODEL * 4 + VMEM_SLACK))),
        name="mixer_out",
    )(y_s, y_a, w_s, w_a, w_o, proj, proj, h, mod3, ln_g, ln_b, mod3, mod3, r_hi, r_lo, r_b)


def _dispatch(ids, n_tok):
    e_flat = ids[:, :MOE_TOP_K].reshape(-1)
    n_assign = n_tok * MOE_TOP_K
    onehot = (e_flat[:, None] == jnp.arange(N_EXPERTS, dtype=jnp.int32)[None, :]).astype(jnp.int32)
    csum = jnp.cumsum(onehot, axis=0)
    rank = jnp.sum(csum * onehot, axis=1) - 1
    counts = csum[-1]
    padded = (counts + MOE_BLOCK - 1) // MOE_BLOCK * MOE_BLOCK
    pend = jnp.cumsum(padded)
    pstarts = pend - padded
    dest = (pstarts[e_flat] + rank).astype(jnp.int32)
    n_blocks = n_assign // MOE_BLOCK + N_EXPERTS
    tok_flat = jnp.arange(n_assign, dtype=jnp.int32) // MOE_TOP_K
    slot_tok = jnp.zeros((n_blocks * MOE_BLOCK,), jnp.int32).at[dest].set(tok_flat)
    block_start = jnp.arange(n_blocks, dtype=jnp.int32) * MOE_BLOCK
    block_exp = jnp.minimum(jnp.sum(block_start[:, None] >= pend[None, :], axis=1), N_EXPERTS - 1).astype(jnp.int32)
    n_used = (pend[-1] // MOE_BLOCK).astype(jnp.int32).reshape(1)
    return dest, slot_tok, block_exp, n_used, n_blocks


def _row_copy(src_hbm, src_row, dst, dst_row, sem):
    return pltpu.make_async_copy(src_hbm.at[pl.ds(src_row, 1)], dst.at[pl.ds(dst_row, 1)], sem)


def _gather_rows(src_hbm, idx_ref, base, dst, sem, n_rows, start, dst_row=lambda r: r):
    for r in range(n_rows):
        cp = _row_copy(src_hbm, idx_ref[base + r], dst, dst_row(r), sem)
        if start:
            cp.start()
        else:
            cp.wait()


def _expert_kernel(bexp_ref, stok_ref, nused_ref, u_hbm, wg_ref, wu_ref, wd_ref, o_ref, xbuf, sem):
    i = pl.program_id(0)
    n = nused_ref[0]
    slot = i % 2

    def gather(blk, s, start):
        _gather_rows(u_hbm, stok_ref, blk * MOE_BLOCK, xbuf.at[s], sem.at[s], MOE_BLOCK, start)

    @pl.when(i == 0)
    def _():
        gather(0, 0, True)

    @pl.when(i < n)
    def _():
        gather(i, slot, False)
        gather(i + 1, 1 - slot, True)
        x = xbuf[slot].astype(BF16)
        hid = _silu(jnp.dot(x, wg_ref[...], preferred_element_type=F32)) * jnp.dot(x, wu_ref[...], preferred_element_type=F32)
        o_ref[...] = jnp.dot(hid.astype(BF16), wd_ref[...], preferred_element_type=F32)

    @pl.when(i == n - 1)
    def _():
        gather(i + 1, 1 - slot, False)

    @pl.when(i >= n)
    def _():
        o_ref[...] = jnp.zeros_like(o_ref)


def _experts(layer, u, block_exp, slot_tok, n_used, n_blocks, w_gate, w_up, w_down):
    wspec_in = pl.BlockSpec((None, None, D_MODEL, EXPERT_FF), lambda i, be, st, nu: (layer, be[i], 0, 0))
    grid_spec = pltpu.PrefetchScalarGridSpec(
        num_scalar_prefetch=3,
        grid=(n_blocks,),
        in_specs=[pl.BlockSpec(memory_space=pl.ANY), wspec_in, wspec_in,
                  pl.BlockSpec((None, None, EXPERT_FF, D_MODEL), lambda i, be, st, nu: (layer, be[i], 0, 0))],
        out_specs=pl.BlockSpec((MOE_BLOCK, D_MODEL), lambda i, be, st, nu: (i, 0)),
        scratch_shapes=[pltpu.VMEM((2, MOE_BLOCK, D_MODEL), F32), pltpu.SemaphoreType.DMA((2,))],
    )
    return pl.pallas_call(
        _expert_kernel,
        out_shape=jax.ShapeDtypeStruct((n_blocks * MOE_BLOCK, D_MODEL), F32),
        grid_spec=grid_spec,
        compiler_params=_params(("arbitrary",), 3 * D_MODEL * EXPERT_FF * 2 + 2 * MOE_BLOCK * D_MODEL * 4),
        name="moe_experts",
    )(block_exp, slot_tok, n_used, u, w_gate, w_up, w_down)


def _combine_kernel(dest_ref, y_hbm, wts_ref, h_ref, gate_ref, lg_ref, lb_ref, sh_ref, sc_ref, h_out, u_out,
                    ybuf, sem, *, alpha, n_tiles):
    i = pl.program_id(0)
    slot = i % 2
    rows = TILE * MOE_TOP_K

    def gather(tile, s, start):
        _gather_rows(y_hbm, dest_ref, tile * rows, ybuf.at[s], sem.at[s], rows, start,
                     dst_row=lambda r: (r % MOE_TOP_K) * TILE + r // MOE_TOP_K)

    @pl.when(i == 0)
    def _():
        gather(0, 0, True)

    ahead = jnp.minimum(i + 1, n_tiles - 1)
    gather(i, slot, False)
    gather(ahead, 1 - slot, True)
    wts = wts_ref[...]
    y = wts[:, 0:1] * ybuf[slot, 0:TILE, :] + wts[:, 1:2] * ybuf[slot, TILE:2 * TILE, :]
    h2 = _ln(alpha * h_ref[...] + gate_ref[...] * y) * lg_ref[...] + lb_ref[...]
    h_out[...] = h2
    u_out[...] = (_ln(h2) * (1.0 + sc_ref[...]) + sh_ref[...]).astype(u_out.dtype)

    @pl.when(i == n_tiles - 1)
    def _():
        gather(ahead, 1 - slot, False)


def _combine(lay, dest, y_slots, wts, h, mod3, mod3_next, ln_g, ln_b, alpha):
    row = pl.BlockSpec((TILE, D_MODEL), lambda i, d: (i, 0))
    vec = pl.BlockSpec((1, D_MODEL), lambda i, d: (0, 0))
    grid_spec = pltpu.PrefetchScalarGridSpec(
        num_scalar_prefetch=1,
        grid=(lay.n_tiles,),
        in_specs=[pl.BlockSpec(memory_space=pl.ANY), pl.BlockSpec((TILE, LANES), lambda i, d: (i, 0)), row,
                  lay.mod_spec(5), vec, vec, lay.mod_spec(0), lay.mod_spec(1)],
        out_specs=(row, row),
        scratch_shapes=[pltpu.VMEM((2, MOE_TOP_K * TILE, D_MODEL), F32), pltpu.SemaphoreType.DMA((2,))],
    )
    return pl.pallas_call(
        functools.partial(_combine_kernel, alpha=alpha, n_tiles=lay.n_tiles),
        out_shape=(jax.ShapeDtypeStruct((lay.m, D_MODEL), F32), jax.ShapeDtypeStruct((lay.m, D_MODEL), BF16)),
        grid_spec=grid_spec,
        compiler_params=_params(("arbitrary",), 5 * TILE * D_MODEL * 4),
        name="moe_combine_residual",
    )(dest, y_slots, wts, h, mod3, ln_g, ln_b, mod3_next, mod3_next)


def _group_lanes(v):
    per_group = v.reshape(2, SSM_GROUPS, HEADS_PER_GROUP).transpose(1, 0, 2).reshape(SSM_GROUPS, 2 * HEADS_PER_GROUP)
    return jnp.pad(per_group, ((0, 0), (0, LANES - 2 * HEADS_PER_GROUP))).reshape(1, SSM_GROUPS * LANES)


def kernel(x, c, ctx, c_ctx, w_ada, b_ada, w_in, conv_w, conv_b, ssm_dt_bias, ssm_a_log, ssm_d, ssm_norm_g, lam_q1, lam_k1, lam_q2, lam_k2, attn_subln_g, w_br_ssm, w_br_att, w_out, ln1_g, ln1_b, w_router_group, b_router_group, w_router_expert, b_router_expert, w_exp_gate, w_exp_up, w_exp_down, ln2_g, ln2_b):
    batch, seq, d = x.shape
    depth = w_ada.shape[0]
    assert d == D_MODEL and batch < MOD_ROWS
    lay = _Layout(batch, seq, ctx.shape[1])
    alpha = (2.0 * depth) ** 0.25

    c_all = jnp.zeros((MOD_ROWS, d), F32).at[:batch].set(c).at[batch].set(c_ctx)
    mod = _adaln(c_all, w_ada, b_ada).reshape(depth, MOD_ROWS, 1, 6 * d)
    tables = _rope_tables(seq)

    w_proj = _regroup_w_in(w_in)
    w_bs, w_ba, w_o = w_br_ssm.astype(BF16), w_br_att.astype(BF16), w_out.astype(BF16)
    w_eg, w_eu, w_ed = w_exp_gate.astype(BF16), w_exp_up.astype(BF16), w_exp_down.astype(BF16)
    w_dt = w_in[:, :, IN_DT:IN_Q].reshape(depth, d, 2, SSM_GROUPS, HEADS_PER_GROUP).transpose(0, 1, 3, 2, 4)
    w_dt = jnp.pad(w_dt.reshape(depth, d, SSM_GROUPS, 2 * HEADS_PER_GROUP),
                   ((0, 0), (0, 0), (0, 0), (0, LANES - 2 * HEADS_PER_GROUP)))
    w_dt = w_dt.reshape(depth, d, SSM_GROUPS * LANES).astype(BF16)

    h = jnp.concatenate([ctx, x], axis=1).reshape(lay.m, d)
    u = _lnmod(lay, h, mod[0], 0, 1, BF16)
    for l in range(depth):
        lam_init = 0.8 - 0.6 * math.exp(-0.3 * l)
        proj = _matmul(u, w_proj, l)
        dt_raw = _matmul(u, w_dt, l)

        xbc = _conv_silu(lay, proj, conv_w[l], conv_b[l])
        y_f, y_r = _ssd(lay, xbc, dt_raw, _group_lanes(ssm_dt_bias[l]), _group_lanes(ssm_a_log[l]))
        out_lay = _LatentLayout(lay) if l == depth - 1 else lay
        y_s = _ssd_out(out_lay, y_f, y_r, xbc, proj, jnp.repeat(ssm_d[l], SSM_HEAD_DIM).reshape(1, d),
                       ssm_norm_g[l].reshape(1, d))

        lam_p = jnp.zeros((8, ATT_HEAD_DIM), F32).at[0].set(lam_q1[l]).at[1].set(lam_k1[l]).at[2].set(lam_q2[l]).at[3].set(lam_k2[l])
        y_a = _attention(lay, proj, tables, lam_p, attn_subln_g[l].reshape(1, LANES), lam_init)

        w_r = jnp.zeros((d, LANES), F32).at[:, :MOE_GROUPS].set(w_router_group[l]).at[:, MOE_GROUPS:MOE_GROUPS + N_EXPERTS].set(w_router_expert[l])
        b_r = jnp.zeros((1, LANES), F32).at[0, :MOE_GROUPS].set(b_router_group[l]).at[0, MOE_GROUPS:MOE_GROUPS + N_EXPERTS].set(b_router_expert[l])
        r_hi, r_lo = _split_bf16(w_r)
        h, u2, ids, wts = _mixer_out(out_lay, l, y_s, y_a, w_bs, w_ba, w_o, proj, h, mod[l], ln1_g[l].reshape(1, d),
                                     ln1_b[l].reshape(1, d), r_hi, r_lo, b_r, alpha)
        dest, slot_tok, block_exp, n_used, n_blocks = _dispatch(ids, out_lay.m)
        y_slots = _experts(l, u2, block_exp, slot_tok, n_used, n_blocks, w_eg, w_eu, w_ed)
        h, u = _combine(out_lay, dest, y_slots, wts, h, mod[l], mod[min(l + 1, depth - 1)],
                        ln2_g[l].reshape(1, d), ln2_b[l].reshape(1, d), alpha)
    return h.reshape(batch, seq, d)
```

```python
import functools
import math

import jax
import jax.numpy as jnp
from jax import lax
from jax.experimental import pallas as pl
from jax.experimental.pallas import tpu as pltpu

F32 = jnp.float32
BF16 = jnp.bfloat16
HIGHEST = lax.Precision.HIGHEST

D_MODEL = 2048
GRID_W = 64

SSM_HEAD_DIM = 64
SSM_HEADS = D_MODEL // SSM_HEAD_DIM
SSM_GROUPS = 4
SSM_STATE = 128
HEADS_PER_GROUP = SSM_HEADS // SSM_GROUPS
GROUP_WIDTH = HEADS_PER_GROUP * SSM_HEAD_DIM
CONV_WIDTH = 5
CONV_DIM = D_MODEL + 2 * SSM_GROUPS * SSM_STATE
SSD_CHUNK = 128
SSD_GROUP_BLOCK = 2

ATT_HEAD_DIM = 64
ATT_HEADS = D_MODEL // (2 * ATT_HEAD_DIM)
ROPE_BASE = 10000.0
ATT_HEAD_BLOCK = 4
ATT_Q_SCALE = ATT_HEAD_DIM ** -0.5 * math.log2(math.e)

MOE_GROUPS = 4
EXPERTS_PER_GROUP = 8
N_EXPERTS = MOE_GROUPS * EXPERTS_PER_GROUP
MOE_TOP_K = 2
EXPERT_FF = D_MODEL // 2

LN_EPS = 1e-6
RMS_EPS = 1e-5

LANES = 128
ROW_TILES = D_MODEL // LANES
MOD_ROWS = 16
TILE = 256
MOE_BLOCK = 512
VMEM_CAP = 60 * 1024 * 1024
VMEM_SLACK = 6 * 1024 * 1024


def _params(sem, block_bytes):
    return pltpu.CompilerParams(dimension_semantics=sem,
                                vmem_limit_bytes=int(min(VMEM_CAP, 2 * block_bytes + VMEM_SLACK)))


def _ln(x):
    mu = jnp.mean(x, axis=-1, keepdims=True)
    xc = x - mu
    var = jnp.mean(xc * xc, axis=-1, keepdims=True)
    return xc * lax.rsqrt(var + LN_EPS)


def _silu(x):
    return x * jax.nn.sigmoid(x)


def _mod_kernel(c_ref, w_ref, b_ref, o_ref):
    sc = _silu(c_ref[...]).astype(BF16)
    o_ref[...] = jnp.dot(sc, w_ref[...].astype(BF16), preferred_element_type=F32) + b_ref[...]


def _adaln(c_all, w_ada, b_ada):
    depth, d, n = w_ada.shape
    tn = 1024
    return pl.pallas_call(
        _mod_kernel,
        out_shape=jax.ShapeDtypeStruct((depth, MOD_ROWS, n), F32),
        grid=(depth, n // tn),
        in_specs=[pl.BlockSpec((MOD_ROWS, d), lambda l, j: (0, 0)),
                  pl.BlockSpec((None, d, tn), lambda l, j: (l, 0, j)),
                  pl.BlockSpec((None, 1, tn), lambda l, j: (l, 0, j))],
        out_specs=pl.BlockSpec((None, MOD_ROWS, tn), lambda l, j: (l, 0, j)),
        compiler_params=_params(("parallel", "parallel"), d * tn * 4 + d * tn * 2),
        name="adaln_mod",
    )(c_all, w_ada, b_ada.reshape(depth, 1, n))


class _Layout:
    def __init__(self, batch, seq, ctx):
        assert ctx % TILE == 0 and seq % TILE == 0 and seq % GRID_W == 0
        self.batch, self.seq, self.ctx = batch, seq, ctx
        self.nt = ctx + seq
        self.m = batch * self.nt
        self.tiles_per_batch = self.nt // TILE
        self.ctx_tiles = ctx // TILE
        self.n_tiles = self.m // TILE

    def mod_row(self, i):
        return jnp.where(i % self.tiles_per_batch < self.ctx_tiles, self.batch, i // self.tiles_per_batch)

    def mod_spec(self, chunk):
        return pl.BlockSpec((None, 1, D_MODEL), lambda i, *_: (self.mod_row(i), 0, chunk))

    def full_tile(self, i):
        return i

    def full_row_spec(self, col_block=0):
        return pl.BlockSpec((TILE, D_MODEL), lambda i, *_: (self.full_tile(i), col_block))


class _LatentLayout(_Layout):
    def __init__(self, full):
        self.batch, self.seq, self.ctx = full.batch, full.seq, full.ctx
        self.m = full.batch * full.seq
        self.tiles_per_batch = full.seq // TILE
        self.n_tiles = self.m // TILE
        self._full = full

    def mod_row(self, i):
        return i // self.tiles_per_batch

    def full_tile(self, i):
        f = self._full
        return (i // self.tiles_per_batch) * f.tiles_per_batch + f.ctx_tiles + i % self.tiles_per_batch


def _lnmod_kernel(h_ref, sh_ref, sc_ref, u_ref):
    u_ref[...] = (_ln(h_ref[...]) * (1.0 + sc_ref[...]) + sh_ref[...]).astype(u_ref.dtype)


def _lnmod(lay, h, mod3, shift_chunk, scale_chunk, out_dtype):
    row = pl.BlockSpec((TILE, D_MODEL), lambda i: (i, 0))
    return pl.pallas_call(
        _lnmod_kernel,
        out_shape=jax.ShapeDtypeStruct((lay.m, D_MODEL), out_dtype),
        grid=(lay.n_tiles,),
        in_specs=[row, lay.mod_spec(shift_chunk), lay.mod_spec(scale_chunk)],
        out_specs=row,
        compiler_params=_params(("parallel",), 2 * TILE * D_MODEL * 4),
        name="ln_modulate",
    )(h, mod3, mod3)


def _mm_kernel(a_ref, w_ref, o_ref):
    o_ref[...] = jnp.dot(a_ref[...], w_ref[...], preferred_element_type=F32).astype(o_ref.dtype)


def _matmul(a, w, layer, out_dtype=F32, tm=1024, tn=1024):
    m, k = a.shape
    n = w.shape[2]
    tn = min(tn, n)
    tm = math.gcd(m, tm)
    assert tm % TILE == 0 and n % tn == 0
    return pl.pallas_call(
        _mm_kernel,
        out_shape=jax.ShapeDtypeStruct((m, n), out_dtype),
        grid=(m // tm, n // tn),
        in_specs=[pl.BlockSpec((tm, k), lambda i, j: (i, 0)),
                  pl.BlockSpec((None, k, tn), lambda i, j: (layer, 0, j))],
        out_specs=pl.BlockSpec((tm, tn), lambda i, j: (i, j)),
        compiler_params=_params(("parallel", "parallel"), tm * k * 2 + k * tn * 2 + tm * tn * 4),
        name="matmul",
    )(a, w)


IN_XBC = D_MODEL
IN_DT = IN_XBC + CONV_DIM
IN_Q = IN_DT + 2 * SSM_HEADS
IN_GATES = IN_Q + 3 * D_MODEL
PROJ_Z = 0
PROJ_GATES = PROJ_Z + D_MODEL
PROJ_Q = PROJ_GATES + 2 * D_MODEL
PROJ_XBC = PROJ_Q + 3 * D_MODEL
PROJ_COLS = PROJ_XBC + CONV_DIM


def _regroup_w_in(w_in):
    return jnp.concatenate([w_in[:, :, :IN_XBC], w_in[:, :, IN_GATES:], w_in[:, :, IN_Q:IN_GATES],
                            w_in[:, :, IN_XBC:IN_DT]], axis=-1).astype(BF16)


def _conv_kernel(x_ref, w_ref, b_ref, o_ref, *, ctx):
    x = x_ref[...]
    nt = x.shape[0]
    t = lax.broadcasted_iota(jnp.int32, x.shape, 0)
    lo = jnp.where(t < ctx, 0, ctx)
    hi = jnp.where(t < ctx, ctx, nt)
    pad = (CONV_WIDTH - 1) // 2
    acc = b_ref[...] + x * w_ref[pad:pad + 1, :]
    for k in range(CONV_WIDTH):
        d = k - pad
        if d == 0:
            continue
        shifted = pltpu.roll(x, (-d) % nt, axis=0)
        valid = (t + d >= lo) & (t + d < hi)
        acc = acc + jnp.where(valid, shifted, 0.0) * w_ref[k:k + 1, :]
    o_ref[...] = _silu(acc)


def _conv_silu(lay, proj, conv_w, conv_b):
    tc = 512
    off = PROJ_XBC // tc
    return pl.pallas_call(
        functools.partial(_conv_kernel, ctx=lay.ctx),
        out_shape=jax.ShapeDtypeStruct((lay.m, CONV_DIM), F32),
        grid=(lay.batch, CONV_DIM // tc),
        in_specs=[pl.BlockSpec((lay.nt, tc), lambda b, j: (b, off + j)),
                  pl.BlockSpec((CONV_WIDTH, tc), lambda b, j: (0, j)),
                  pl.BlockSpec((1, tc), lambda b, j: (0, j))],
        out_specs=pl.BlockSpec((lay.nt, tc), lambda b, j: (b, j)),
        compiler_params=_params(("parallel", "parallel"), 6 * lay.nt * tc * 4),
        name="conv_silu",
    )(proj, conv_w, conv_b.reshape(1, CONV_DIM))


def _ssd_direction(x_ref, b_ref, c_ref, dt_ref, bias_ref, alog_ref, st_ref, y_ref, *, lane_off, upper):
    L = SSD_CHUNK
    r = lax.broadcasted_iota(jnp.int32, (L, L), 0)
    s = lax.broadcasted_iota(jnp.int32, (L, L), 1)
    tri = (s >= r) if upper else (s <= r)
    x = x_ref[...]
    bm = b_ref[...]
    cm = c_ref[...].astype(BF16)
    dtv = dt_ref[...] + bias_ref[...]
    dt = jnp.maximum(dtv, 0.0) + jnp.log(1.0 + jnp.exp(-jnp.abs(dtv)))
    adt = dt * (-jnp.exp(alog_ref[...]))
    cs = jnp.dot(tri.astype(F32), adt, precision=HIGHEST, preferred_element_type=F32)
    cs_t = cs.T
    a_tot = jnp.sum(adt, axis=0, keepdims=True)
    cb = lax.dot_general(cm, bm.astype(BF16), (((1,), (1,)), ((), ())), preferred_element_type=F32)
    y_off = jnp.dot(cm, st_ref[...].astype(BF16), preferred_element_type=F32)
    bm_t = bm.T.astype(BF16)
    first_half = lax.broadcasted_iota(jnp.int32, (L, LANES), 1) < SSM_HEAD_DIM
    first_half_row = first_half[0:1, :]
    for pair in range(HEADS_PER_GROUP // 2):
        la = lane_off + 2 * pair
        lb = la + 1
        cols = slice(pair * LANES, (pair + 1) * LANES)
        cs_pair = jnp.where(first_half, cs[:, la:la + 1], cs[:, lb:lb + 1])
        dt_pair = jnp.where(first_half, dt[:, la:la + 1], dt[:, lb:lb + 1])
        tot_pair = jnp.where(first_half_row, a_tot[:, la:la + 1], a_tot[:, lb:lb + 1])
        xdt = x[:, cols] * dt_pair
        y_pair = y_off[:, cols] * jnp.exp(cs_pair)
        for lane, keep in ((la, first_half), (lb, jnp.logical_not(first_half))):
            seg = cs[:, lane:lane + 1] - cs_t[lane:lane + 1, :]
            decay = jnp.where(tri, jnp.exp(jnp.where(tri, seg, 0.0)), 0.0)
            mh = (cb * decay).astype(BF16)
            y_pair = y_pair + jnp.dot(mh, jnp.where(keep, xdt, 0.0).astype(BF16), preferred_element_type=F32)
        y_ref[:, cols] = y_pair
        z = (xdt * jnp.exp(tot_pair - cs_pair)).astype(BF16)
        st_ref[:, cols] = st_ref[:, cols] * jnp.exp(tot_pair) + jnp.dot(bm_t, z, preferred_element_type=F32)


def _ssd_kernel(xf, bf, cf, dtf, xr, br, cr, dtr, bias_ref, alog_ref, yf_ref, yr_ref, stf_ref, str_ref):
    @pl.when(pl.program_id(2) == 0)
    def _():
        stf_ref[...] = jnp.zeros_like(stf_ref)
        str_ref[...] = jnp.zeros_like(str_ref)

    for g in range(SSD_GROUP_BLOCK):
        wide = (slice(None), slice(g * GROUP_WIDTH, (g + 1) * GROUP_WIDTH))
        lanes = (slice(None), slice(g * LANES, (g + 1) * LANES))
        small = (bias_ref.at[lanes], alog_ref.at[lanes])
        _ssd_direction(xf.at[wide], bf.at[lanes], cf.at[lanes], dtf.at[lanes], *small, stf_ref.at[wide], yf_ref.at[wide],
                       lane_off=0, upper=False)
        _ssd_direction(xr.at[wide], br.at[lanes], cr.at[lanes], dtr.at[lanes], *small, str_ref.at[wide], yr_ref.at[wide],
                       lane_off=HEADS_PER_GROUP, upper=True)


def _ssd(lay, xbc, dt_raw, dt_bias_g, a_log_g):
    L = SSD_CHUNK
    nc = lay.nt // L
    ctx_chunks = lay.ctx // L
    gb = SSD_GROUP_BLOCK
    b_col = D_MODEL // (gb * SSM_STATE)
    c_col = b_col + SSM_GROUPS // gb

    def fwd(t):
        return t

    def rev(t):
        return jnp.where(t < ctx_chunks, ctx_chunks - 1 - t, nc - 1 - (t - ctx_chunks))

    def specs(order):
        return [pl.BlockSpec((L, gb * GROUP_WIDTH), lambda b, g, t: (b * nc + order(t), g)),
                pl.BlockSpec((L, gb * SSM_STATE), lambda b, g, t: (b * nc + order(t), b_col + g)),
                pl.BlockSpec((L, gb * SSM_STATE), lambda b, g, t: (b * nc + order(t), c_col + g)),
                pl.BlockSpec((L, gb * LANES), lambda b, g, t: (b * nc + order(t), g))]

    small = pl.BlockSpec((1, gb * LANES), lambda b, g, t: (0, g))
    y_shape = jax.ShapeDtypeStruct((lay.m, D_MODEL), F32)
    state = pltpu.VMEM((SSM_STATE, gb * GROUP_WIDTH), F32)
    return pl.pallas_call(
        _ssd_kernel,
        out_shape=(y_shape, y_shape),
        grid=(lay.batch, SSM_GROUPS // gb, nc),
        in_specs=specs(fwd) + specs(rev) + [small, small],
        out_specs=(pl.BlockSpec((L, gb * GROUP_WIDTH), lambda b, g, t: (b * nc + fwd(t), g)),
                   pl.BlockSpec((L, gb * GROUP_WIDTH), lambda b, g, t: (b * nc + rev(t), g))),
        scratch_shapes=[state, state],
        compiler_params=_params(("parallel", "parallel", "arbitrary"), 8 * L * gb * GROUP_WIDTH * 4),
        name="ssd_scan",
    )(xbc, xbc, xbc, dt_raw, xbc, xbc, xbc, dt_raw, dt_bias_g, a_log_g)


def _ssd_out_kernel(yf_ref, yr_ref, x_ref, z_ref, d_ref, g_ref, o_ref):
    y = d_ref[...] * x_ref[...] + yf_ref[...] + yr_ref[...]
    yz = y * _silu(z_ref[...])
    for g in range(SSM_GROUPS):
        cols = slice(g * GROUP_WIDTH, (g + 1) * GROUP_WIDTH)
        v = yz[:, cols]
        inv = lax.rsqrt(jnp.mean(v * v, axis=-1, keepdims=True) + RMS_EPS)
        o_ref[:, cols] = (v * inv * g_ref[:, cols]).astype(o_ref.dtype)


def _ssd_out(lay, y_f, y_r, xbc, proj, d_exp, norm_g):
    src = lay.full_row_spec()
    vec = pl.BlockSpec((1, D_MODEL), lambda i: (0, 0))
    return pl.pallas_call(
        _ssd_out_kernel,
        out_shape=jax.ShapeDtypeStruct((lay.m, D_MODEL), BF16),
        grid=(lay.n_tiles,),
        in_specs=[src, src, src, src, vec, vec],
        out_specs=pl.BlockSpec((TILE, D_MODEL), lambda i: (i, 0)),
        compiler_params=_params(("parallel",), 6 * TILE * D_MODEL * 4),
        name="ssd_gated_norm",
    )(y_f, y_r, xbc, proj, d_exp, norm_g)


def _rope_tables(seq):
    half = ATT_HEAD_DIM // 2
    nf = half // 2
    inv = ROPE_BASE ** (-(jnp.arange(nf, dtype=F32) / nf))
    rows = seq // GRID_W
    row = jnp.repeat(jnp.arange(rows, dtype=F32), GRID_W)
    col = jnp.tile(jnp.arange(GRID_W, dtype=F32), rows)
    lane = jnp.arange(2 * ATT_HEAD_DIM)
    freq = inv[lane % nf]
    pos = jnp.where((lane % ATT_HEAD_DIM) < half, row[:, None], col[:, None])
    ang = pos * freq[None, :]
    low = (lane % half) < nf
    sin = jnp.sin(ang)
    return jnp.cos(ang), jnp.where(low, -sin, 0.0), jnp.where(low, 0.0, sin)


def _rope(t, cos, sin_lo, sin_hi):
    nf = ATT_HEAD_DIM // 4
    return t * cos + pltpu.roll(t, LANES - nf, axis=1) * sin_lo + pltpu.roll(t, nf, axis=1) * sin_hi


def _dot_nt(a, b):
    return lax.dot_general(a, b, (((1,), (1,)), ((), ())), preferred_element_type=F32)


def _attn_kernel(q_ref, k_ref, v_ref, cos_ref, slo_ref, shi_ref, lam_ref, g_ref, o_ref,
                 krot_ref, kctx_ref, vb_ref, *, ctx, ctx_tiles, lam_init):
    qi = pl.program_id(2)
    heads = [slice(hh * LANES, (hh + 1) * LANES) for hh in range(ATT_HEAD_BLOCK)]

    @pl.when(qi == 0)
    def _():
        for cols in heads:
            krot_ref[:, cols] = _rope(k_ref[ctx:, cols], cos_ref[...], slo_ref[...], shi_ref[...]).astype(BF16)
        kctx_ref[...] = k_ref[:ctx, :].astype(BF16)
        vb_ref[...] = v_ref[...].astype(BF16)

    lp = lam_ref[...]
    lam = (jnp.exp(jnp.sum(lp[0:1] * lp[1:2], axis=-1, keepdims=True))
           - jnp.exp(jnp.sum(lp[2:3] * lp[3:4], axis=-1, keepdims=True)) + lam_init)
    first_map = lax.broadcasted_iota(jnp.int32, (TILE, LANES), 1) < ATT_HEAD_DIM

    def split(q):
        return jnp.where(first_map, q, 0.0).astype(BF16), jnp.where(first_map, 0.0, q).astype(BF16)

    def softmax_pv(queries, keys, values):
        scores = [_dot_nt(q, k) for q, k in zip(queries, keys)]
        mx = functools.reduce(jnp.maximum, [jnp.max(s, axis=-1, keepdims=True) for s in scores])
        e = [jnp.exp2(s - mx) for s in scores]
        total = sum(jnp.sum(x, axis=-1, keepdims=True) for x in e)
        pv = sum(jnp.dot(x.astype(BF16), v, preferred_element_type=F32) for x, v in zip(e, values))
        return pv * (1.0 / total)

    def attend(cols, queries, keys, values):
        o = softmax_pv(queries[0], keys, values) - lam * softmax_pv(queries[1], keys, values)
        inv = lax.rsqrt(jnp.mean(o * o, axis=-1, keepdims=True) + RMS_EPS)
        o_ref[:, cols] = (o * inv * g_ref[...] * (1.0 - lam_init)).astype(o_ref.dtype)

    @pl.when(qi < ctx_tiles)
    def _():
        for cols in heads:
            qn = split(q_ref[:, cols] * ATT_Q_SCALE)
            attend(cols, [[qn[0]], [qn[1]]], [kctx_ref[:, cols]], [vb_ref[:ctx, cols]])

    @pl.when(qi >= ctx_tiles)
    def _():
        start = pl.multiple_of((qi - ctx_tiles) * TILE, TILE)
        rows = pl.ds(start, TILE)
        for cols in heads:
            q = q_ref[:, cols] * ATT_Q_SCALE
            qn = split(q)
            qr = split(_rope(q, cos_ref[rows, :], slo_ref[rows, :], shi_ref[rows, :]))
            attend(cols, [[qn[0], qr[0]], [qn[1], qr[1]]], [kctx_ref[:, cols], krot_ref[:, cols]],
                   [vb_ref[:ctx, cols], vb_ref[ctx:, cols]])


def _attention(lay, qkv, tables, lam_p, subln_g, lam_init):
    width = ATT_HEAD_BLOCK * LANES
    hb = D_MODEL // width
    q0 = PROJ_Q // width
    tab = pl.BlockSpec((lay.seq, LANES), lambda b, h, i: (0, 0))
    kv_bytes = lay.nt * width * 4
    return pl.pallas_call(
        functools.partial(_attn_kernel, ctx=lay.ctx, ctx_tiles=lay.ctx_tiles, lam_init=lam_init),
        out_shape=jax.ShapeDtypeStruct((lay.m, D_MODEL), BF16),
        grid=(lay.batch, ATT_HEADS // ATT_HEAD_BLOCK, lay.tiles_per_batch),
        in_specs=[pl.BlockSpec((TILE, width), lambda b, h, i: (b * lay.tiles_per_batch + i, q0 + h)),
                  pl.BlockSpec((lay.nt, width), lambda b, h, i: (b, q0 + hb + h)),
                  pl.BlockSpec((lay.nt, width), lambda b, h, i: (b, q0 + 2 * hb + h)),
                  tab, tab, tab,
                  pl.BlockSpec((8, ATT_HEAD_DIM), lambda b, h, i: (0, 0)),
                  pl.BlockSpec((1, LANES), lambda b, h, i: (0, 0))],
        out_specs=pl.BlockSpec((TILE, width), lambda b, h, i: (b * lay.tiles_per_batch + i, h)),
        scratch_shapes=[pltpu.VMEM((lay.seq, width), BF16), pltpu.VMEM((lay.ctx, width), BF16),
                        pltpu.VMEM((lay.nt, width), BF16)],
        compiler_params=_params(("parallel", "parallel", "arbitrary"),
                                2 * kv_bytes + 3 * lay.seq * LANES * 4 + 4 * TILE * lay.nt * 4),
        name="diff_attention",
    )(qkv, qkv, qkv, *tables, lam_p, subln_g)


def _route(logits):
    lane_i = lax.broadcasted_iota(jnp.int32, logits.shape, 1)
    lane = lane_i.astype(F32)
    none = float(LANES)
    neg = -jnp.inf

    def softmax(mask):
        v = jnp.where(mask, logits, neg)
        e = jnp.exp(v - jnp.max(v, axis=-1, keepdims=True))
        return e / jnp.sum(e, axis=-1, keepdims=True)

    def top(p, mask):
        best = jnp.max(jnp.where(mask, p, -1.0), axis=-1, keepdims=True)
        idx = jnp.min(jnp.where(mask & (p == best), lane, none), axis=-1, keepdims=True)
        return best, idx

    is_group = lane_i < MOE_GROUPS
    p_grp, grp = top(softmax(is_group), is_group)
    first = MOE_GROUPS + grp * EXPERTS_PER_GROUP
    in_grp = (lane >= first) & (lane < first + EXPERTS_PER_GROUP)
    pe = softmax(in_grp)
    p1, i1 = top(pe, in_grp)
    rest = in_grp & (lane != i1)
    p2, i2 = top(pe, rest)
    denom = p1 + p2
    w1 = p_grp * p1 / denom
    w2 = p_grp * p2 / denom
    ids = jnp.where(lane_i == 0, i1 - MOE_GROUPS, jnp.where(lane_i == 1, i2 - MOE_GROUPS, 0.0))
    return ids, jnp.where(lane_i == 0, w1, jnp.where(lane_i == 1, w2, 0.0))


def _split_bf16(x):
    hi = x.astype(BF16)
    return hi, (x - hi.astype(F32)).astype(BF16)


def _mixer_out_kernel(ys_ref, ya_ref, ws_ref, wa_ref, wo_ref, gs_ref, ga_ref, h_ref, gate_ref, lg_ref, lb_ref,
                      sh_ref, sc_ref, rh_ref, rl_ref, rb_ref, h_out, u_out, ids_out, wts_out, *, alpha):
    ps = jnp.dot(ys_ref[...], ws_ref[...], preferred_element_type=F32)
    pa = jnp.dot(ya_ref[...], wa_ref[...], preferred_element_type=F32)
    m = (jax.nn.sigmoid(gs_ref[...]) * ps + jax.nn.sigmoid(ga_ref[...]) * pa).astype(BF16)
    o = jnp.dot(m, wo_ref[...], preferred_element_type=F32)
    h1 = _ln(alpha * h_ref[...] + gate_ref[...] * o) * lg_ref[...] + lb_ref[...]
    h_out[...] = h1
    u = _ln(h1) * (1.0 + sc_ref[...]) + sh_ref[...]
    u_hi, u_lo = _split_bf16(u)
    u_out[...] = u_hi.reshape(u_out.shape)
    logits = (jnp.dot(u_hi, rh_ref[...], preferred_element_type=F32)
              + (jnp.dot(u_hi, rl_ref[...], preferred_element_type=F32)
                 + jnp.dot(u_lo, rh_ref[...], preferred_element_type=F32))) + rb_ref[...]
    ids, wts = _route(logits)
    ids_out[...] = ids.astype(jnp.int32)
    wts_out[...] = wts


def _mixer_out(lay, layer, y_s, y_a, w_s, w_a, w_o, proj, h, mod3, ln_g, ln_b, r_hi, r_lo, r_b, alpha):
    row = pl.BlockSpec((TILE, D_MODEL), lambda i: (i, 0))
    vec = pl.BlockSpec((1, D_MODEL), lambda i: (0, 0))
    weight = pl.BlockSpec((None, D_MODEL, D_MODEL), lambda i: (layer, 0, 0), pipeline_mode=pl.Buffered(1))
    rw = pl.BlockSpec((D_MODEL, LANES), lambda i: (0, 0))
    small = pl.BlockSpec((TILE, LANES), lambda i: (i, 0))
    g0 = PROJ_GATES // D_MODEL
    return pl.pallas_call(
        functools.partial(_mixer_out_kernel, alpha=alpha),
        out_shape=(jax.ShapeDtypeStruct((lay.m, D_MODEL), F32), jax.ShapeDtypeStruct((lay.m, ROW_TILES, LANES), BF16),
                   jax.ShapeDtypeStruct((lay.m, LANES), jnp.int32), jax.ShapeDtypeStruct((lay.m, LANES), F32)),
        grid=(lay.n_tiles,),
        in_specs=[row, lay.full_row_spec(), weight, weight, weight,
                  lay.full_row_spec(g0), lay.full_row_spec(g0 + 1),
                  lay.full_row_spec(), lay.mod_spec(2), vec, vec, lay.mod_spec(3), lay.mod_spec(4),
                  rw, rw, pl.BlockSpec((1, LANES), lambda i: (0, 0))],
        out_specs=(row, pl.BlockSpec((TILE, ROW_TILES, LANES), lambda i: (i, 0, 0)), small, small),
        compiler_params=pltpu.CompilerParams(
            dimension_semantics=("parallel",),
            vmem_limit_bytes=int(min(VMEM_CAP, 3 * D_MODEL * D_MODEL * 2 + 2 * 7 * TILE * D_MODEL * 4 + VMEM_SLACK))),
        name="mixer_out",
    )(y_s, y_a, w_s, w_a, w_o, proj, proj, h, mod3, ln_g, ln_b, mod3, mod3, r_hi, r_lo, r_b)


def _dispatch(ids, n_tok):
    e_flat = ids[:, :MOE_TOP_K].reshape(-1)
    n_assign = n_tok * MOE_TOP_K
    onehot = (e_flat[:, None] == jnp.arange(N_EXPERTS, dtype=jnp.int32)[None, :]).astype(jnp.int32)
    csum = jnp.cumsum(onehot, axis=0)
    rank = jnp.sum(csum * onehot, axis=1) - 1
    counts = csum[-1]
    padded = (counts + MOE_BLOCK - 1) // MOE_BLOCK * MOE_BLOCK
    pend = jnp.cumsum(padded)
    pstarts = pend - padded
    dest = (pstarts[e_flat] + rank).astype(jnp.int32)
    n_blocks = n_assign // MOE_BLOCK + N_EXPERTS
    tok_flat = jnp.arange(n_assign, dtype=jnp.int32) // MOE_TOP_K
    slot_tok = jnp.zeros((n_blocks * MOE_BLOCK,), jnp.int32).at[dest].set(tok_flat)
    block_start = jnp.arange(n_blocks, dtype=jnp.int32) * MOE_BLOCK
    block_exp = jnp.minimum(jnp.sum(block_start[:, None] >= pend[None, :], axis=1), N_EXPERTS - 1).astype(jnp.int32)
    n_used = (pend[-1] // MOE_BLOCK).astype(jnp.int32).reshape(1)
    return dest, slot_tok, block_exp, n_used, n_blocks


def _row_copy(src_hbm, src_row, dst, dst_row, sem):
    return pltpu.make_async_copy(src_hbm.at[pl.ds(src_row, 1)], dst.at[pl.ds(dst_row, 1)], sem)


def _gather_rows(src_hbm, idx_ref, base, dst, sem, n_rows, start, dst_row=lambda r: r):
    for r in range(n_rows):
        cp = _row_copy(src_hbm, idx_ref[base + r], dst, dst_row(r), sem)
        if start:
            cp.start()
        else:
            cp.wait()


def _expert_kernel(bexp_ref, stok_ref, nused_ref, u_hbm, wg_ref, wu_ref, wd_ref, o_ref, xbuf, sem):
    i = pl.program_id(0)
    n = nused_ref[0]
    slot = i % 2

    def gather(blk, s, start):
        _gather_rows(u_hbm, stok_ref, blk * MOE_BLOCK, xbuf.at[s], sem.at[s], MOE_BLOCK, start)

    @pl.when(i == 0)
    def _():
        gather(0, 0, True)

    @pl.when(i < n)
    def _():
        gather(i, slot, False)
        gather(i + 1, 1 - slot, True)
        x = xbuf[slot].reshape(MOE_BLOCK, D_MODEL)
        hid = _silu(jnp.dot(x, wg_ref[...], preferred_element_type=F32)) * jnp.dot(x, wu_ref[...], preferred_element_type=F32)
        y = jnp.dot(hid.astype(BF16), wd_ref[...], preferred_element_type=F32)
        o_ref[...] = y.reshape(o_ref.shape)

    @pl.when(i == n - 1)
    def _():
        gather(i + 1, 1 - slot, False)

    @pl.when(i >= n)
    def _():
        o_ref[...] = jnp.zeros_like(o_ref)


def _experts(layer, u, block_exp, slot_tok, n_used, n_blocks, w_gate, w_up, w_down):
    wspec_in = pl.BlockSpec((None, None, D_MODEL, EXPERT_FF), lambda i, be, st, nu: (layer, be[i], 0, 0))
    grid_spec = pltpu.PrefetchScalarGridSpec(
        num_scalar_prefetch=3,
        grid=(n_blocks,),
        in_specs=[pl.BlockSpec(memory_space=pl.ANY), wspec_in, wspec_in,
                  pl.BlockSpec((None, None, EXPERT_FF, D_MODEL), lambda i, be, st, nu: (layer, be[i], 0, 0))],
        out_specs=pl.BlockSpec((MOE_BLOCK, ROW_TILES, LANES), lambda i, be, st, nu: (i, 0, 0)),
        scratch_shapes=[pltpu.VMEM((2, MOE_BLOCK, ROW_TILES, LANES), BF16), pltpu.SemaphoreType.DMA((2,))],
    )
    return pl.pallas_call(
        _expert_kernel,
        out_shape=jax.ShapeDtypeStruct((n_blocks * MOE_BLOCK, ROW_TILES, LANES), F32),
        grid_spec=grid_spec,
        compiler_params=_params(("arbitrary",), 3 * D_MODEL * EXPERT_FF * 2 + 2 * MOE_BLOCK * D_MODEL * 4),
        name="moe_experts",
    )(block_exp, slot_tok, n_used, u, w_gate, w_up, w_down)


def _combine_kernel(dest_ref, y_hbm, wts_ref, h_ref, gate_ref, lg_ref, lb_ref, sh_ref, sc_ref, h_out, u_out,
                    ybuf, sem, *, alpha, n_tiles):
    i = pl.program_id(0)
    slot = i % 2
    rows = TILE * MOE_TOP_K

    def gather(tile, s, start):
        _gather_rows(y_hbm, dest_ref, tile * rows, ybuf.at[s], sem.at[s], rows, start,
                     dst_row=lambda r: (r % MOE_TOP_K) * TILE + r // MOE_TOP_K)

    @pl.when(i == 0)
    def _():
        gather(0, 0, True)

    ahead = jnp.minimum(i + 1, n_tiles - 1)
    gather(i, slot, False)
    gather(ahead, 1 - slot, True)
    wts = wts_ref[...]
    y = (wts[:, 0:1] * ybuf[slot, 0:TILE].reshape(TILE, D_MODEL)
         + wts[:, 1:2] * ybuf[slot, TILE:2 * TILE].reshape(TILE, D_MODEL))
    h2 = _ln(alpha * h_ref[...] + gate_ref[...] * y) * lg_ref[...] + lb_ref[...]
    h_out[...] = h2
    u_out[...] = (_ln(h2) * (1.0 + sc_ref[...]) + sh_ref[...]).astype(u_out.dtype)

    @pl.when(i == n_tiles - 1)
    def _():
        gather(ahead, 1 - slot, False)


def _combine(lay, dest, y_slots, wts, h, mod3, mod3_next, ln_g, ln_b, alpha):
    row = pl.BlockSpec((TILE, D_MODEL), lambda i, d: (i, 0))
    vec = pl.BlockSpec((1, D_MODEL), lambda i, d: (0, 0))
    grid_spec = pltpu.PrefetchScalarGridSpec(
        num_scalar_prefetch=1,
        grid=(lay.n_tiles,),
        in_specs=[pl.BlockSpec(memory_space=pl.ANY), pl.BlockSpec((TILE, LANES), lambda i, d: (i, 0)), row,
                  lay.mod_spec(5), vec, vec, lay.mod_spec(0), lay.mod_spec(1)],
        out_specs=(row, row),
        scratch_shapes=[pltpu.VMEM((2, MOE_TOP_K * TILE, ROW_TILES, LANES), F32), pltpu.SemaphoreType.DMA((2,))],
    )
    return pl.pallas_call(
        functools.partial(_combine_kernel, alpha=alpha, n_tiles=lay.n_tiles),
        out_shape=(jax.ShapeDtypeStruct((lay.m, D_MODEL), F32), jax.ShapeDtypeStruct((lay.m, D_MODEL), BF16)),
        grid_spec=grid_spec,
        compiler_params=_params(("arbitrary",), 5 * TILE * D_MODEL * 4),
        name="moe_combine_residual",
    )(dest, y_slots, wts, h, mod3, ln_g, ln_b, mod3_next, mod3_next)


def _group_lanes(v):
    per_group = v.reshape(2, SSM_GROUPS, HEADS_PER_GROUP).transpose(1, 0, 2).reshape(SSM_GROUPS, 2 * HEADS_PER_GROUP)
    return jnp.pad(per_group, ((0, 0), (0, LANES - 2 * HEADS_PER_GROUP))).reshape(1, SSM_GROUPS * LANES)


def kernel(x, c, ctx, c_ctx, w_ada, b_ada, w_in, conv_w, conv_b, ssm_dt_bias, ssm_a_log, ssm_d, ssm_norm_g, lam_q1, lam_k1, lam_q2, lam_k2, attn_subln_g, w_br_ssm, w_br_att, w_out, ln1_g, ln1_b, w_router_group, b_router_group, w_router_expert, b_router_expert, w_exp_gate, w_exp_up, w_exp_down, ln2_g, ln2_b):
    batch, seq, d = x.shape
    depth = w_ada.shape[0]
    assert d == D_MODEL and batch < MOD_ROWS
    lay = _Layout(batch, seq, ctx.shape[1])
    alpha = (2.0 * depth) ** 0.25

    c_all = jnp.zeros((MOD_ROWS, d), F32).at[:batch].set(c).at[batch].set(c_ctx)
    mod = _adaln(c_all, w_ada, b_ada).reshape(depth, MOD_ROWS, 1, 6 * d)
    tables = _rope_tables(seq)

    w_proj = _regroup_w_in(w_in)
    w_bs, w_ba, w_o = w_br_ssm.astype(BF16), w_br_att.astype(BF16), w_out.astype(BF16)
    w_eg, w_eu, w_ed = w_exp_gate.astype(BF16), w_exp_up.astype(BF16), w_exp_down.astype(BF16)
    w_dt = w_in[:, :, IN_DT:IN_Q].reshape(depth, d, 2, SSM_GROUPS, HEADS_PER_GROUP).transpose(0, 1, 3, 2, 4)
    w_dt = jnp.pad(w_dt.reshape(depth, d, SSM_GROUPS, 2 * HEADS_PER_GROUP),
                   ((0, 0), (0, 0), (0, 0), (0, LANES - 2 * HEADS_PER_GROUP)))
    w_dt = w_dt.reshape(depth, d, SSM_GROUPS * LANES).astype(BF16)

    h = jnp.concatenate([ctx, x], axis=1).reshape(lay.m, d)
    u = _lnmod(lay, h, mod[0], 0, 1, BF16)
    for l in range(depth):
        lam_init = 0.8 - 0.6 * math.exp(-0.3 * l)
        proj = _matmul(u, w_proj, l)
        dt_raw = _matmul(u, w_dt, l)

        xbc = _conv_silu(lay, proj, conv_w[l], conv_b[l])
        y_f, y_r = _ssd(lay, xbc, dt_raw, _group_lanes(ssm_dt_bias[l]), _group_lanes(ssm_a_log[l]))
        out_lay = _LatentLayout(lay) if l == depth - 1 else lay
        y_s = _ssd_out(out_lay, y_f, y_r, xbc, proj, jnp.repeat(ssm_d[l], SSM_HEAD_DIM).reshape(1, d),
                       ssm_norm_g[l].reshape(1, d))

        lam_p = jnp.zeros((8, ATT_HEAD_DIM), F32).at[0].set(lam_q1[l]).at[1].set(lam_k1[l]).at[2].set(lam_q2[l]).at[3].set(lam_k2[l])
        y_a = _attention(lay, proj, tables, lam_p, attn_subln_g[l].reshape(1, LANES), lam_init)

        w_r = jnp.zeros((d, LANES), F32).at[:, :MOE_GROUPS].set(w_router_group[l]).at[:, MOE_GROUPS:MOE_GROUPS + N_EXPERTS].set(w_router_expert[l])
        b_r = jnp.zeros((1, LANES), F32).at[0, :MOE_GROUPS].set(b_router_group[l]).at[0, MOE_GROUPS:MOE_GROUPS + N_EXPERTS].set(b_router_expert[l])
        r_hi, r_lo = _split_bf16(w_r)
        h, u2, ids, wts = _mixer_out(out_lay, l, y_s, y_a, w_bs, w_ba, w_o, proj, h, mod[l], ln1_g[l].reshape(1, d),
                                     ln1_b[l].reshape(1, d), r_hi, r_lo, b_r, alpha)
        dest, slot_tok, block_exp, n_used, n_blocks = _dispatch(ids, out_lay.m)
        y_slots = _experts(l, u2, block_exp, slot_tok, n_used, n_blocks, w_eg, w_eu, w_ed)
        h, u = _combine(out_lay, dest, y_slots, wts, h, mod[l], mod[min(l + 1, depth - 1)],
                        ln2_g[l].reshape(1, d), ln2_b[l].reshape(1, d), alpha)
    return h.reshape(batch, seq, d)
```

```python
import functools
import math

import jax
import jax.numpy as jnp
from jax import lax
from jax.experimental import pallas as pl
from jax.experimental.pallas import tpu as pltpu

F32 = jnp.float32
BF16 = jnp.bfloat16
HIGHEST = lax.Precision.HIGHEST

D_MODEL = 2048
GRID_W = 64

SSM_HEAD_DIM = 64
SSM_HEADS = D_MODEL // SSM_HEAD_DIM
SSM_GROUPS = 4
SSM_STATE = 128
HEADS_PER_GROUP = SSM_HEADS // SSM_GROUPS
GROUP_WIDTH = HEADS_PER_GROUP * SSM_HEAD_DIM
CONV_WIDTH = 5
CONV_DIM = D_MODEL + 2 * SSM_GROUPS * SSM_STATE
SSD_CHUNK = 128
SSD_GROUP_BLOCK = 4

ATT_HEAD_DIM = 64
ATT_HEADS = D_MODEL // (2 * ATT_HEAD_DIM)
ROPE_BASE = 10000.0
ATT_HEAD_BLOCK = 4
ATT_Q_SCALE = ATT_HEAD_DIM ** -0.5 * math.log2(math.e)

MOE_GROUPS = 4
EXPERTS_PER_GROUP = 8
N_EXPERTS = MOE_GROUPS * EXPERTS_PER_GROUP
MOE_TOP_K = 2
EXPERT_FF = D_MODEL // 2

LN_EPS = 1e-6
RMS_EPS = 1e-5

LANES = 128
ROW_TILES = D_MODEL // LANES
MOD_ROWS = 16
TILE = 256
MOE_BLOCK = 128
VMEM_CAP = 60 * 1024 * 1024
VMEM_SLACK = 6 * 1024 * 1024


def _params(sem, block_bytes):
    return pltpu.CompilerParams(dimension_semantics=sem,
                                vmem_limit_bytes=int(min(VMEM_CAP, 2 * block_bytes + VMEM_SLACK)))


def _ln(x):
    mu = jnp.mean(x, axis=-1, keepdims=True)
    xc = x - mu
    var = jnp.mean(xc * xc, axis=-1, keepdims=True)
    return xc * lax.rsqrt(var + LN_EPS)


def _silu(x):
    return x * jax.nn.sigmoid(x)


def _mod_kernel(c_ref, w_ref, b_ref, o_ref):
    sc = _silu(c_ref[...]).astype(BF16)
    o_ref[...] = jnp.dot(sc, w_ref[...].astype(BF16), preferred_element_type=F32) + b_ref[...]


def _adaln(c_all, w_ada, b_ada):
    depth, d, n = w_ada.shape
    tn = 1024
    return pl.pallas_call(
        _mod_kernel,
        out_shape=jax.ShapeDtypeStruct((depth, MOD_ROWS, n), F32),
        grid=(depth, n // tn),
        in_specs=[pl.BlockSpec((MOD_ROWS, d), lambda l, j: (0, 0)),
                  pl.BlockSpec((None, d, tn), lambda l, j: (l, 0, j)),
                  pl.BlockSpec((None, 1, tn), lambda l, j: (l, 0, j))],
        out_specs=pl.BlockSpec((None, MOD_ROWS, tn), lambda l, j: (l, 0, j)),
        compiler_params=_params(("parallel", "parallel"), d * tn * 4 + d * tn * 2),
        name="adaln_mod",
    )(c_all, w_ada, b_ada.reshape(depth, 1, n))


class _Layout:
    def __init__(self, batch, seq, ctx):
        assert ctx % TILE == 0 and seq % TILE == 0 and seq % GRID_W == 0
        self.batch, self.seq, self.ctx = batch, seq, ctx
        self.nt = ctx + seq
        self.m = batch * self.nt
        self.tiles_per_batch = self.nt // TILE
        self.ctx_tiles = ctx // TILE
        self.n_tiles = self.m // TILE

    def mod_row(self, i):
        return jnp.where(i % self.tiles_per_batch < self.ctx_tiles, self.batch, i // self.tiles_per_batch)

    def mod_spec(self, chunk):
        return pl.BlockSpec((None, 1, D_MODEL), lambda i, *_: (self.mod_row(i), 0, chunk))

    def full_tile(self, i):
        return i

    def full_row_spec(self, col_block=0):
        return pl.BlockSpec((TILE, D_MODEL), lambda i, *_: (self.full_tile(i), col_block))


class _LatentLayout(_Layout):
    def __init__(self, full):
        self.batch, self.seq, self.ctx = full.batch, full.seq, full.ctx
        self.m = full.batch * full.seq
        self.tiles_per_batch = full.seq // TILE
        self.n_tiles = self.m // TILE
        self._full = full

    def mod_row(self, i):
        return i // self.tiles_per_batch

    def full_tile(self, i):
        f = self._full
        return (i // self.tiles_per_batch) * f.tiles_per_batch + f.ctx_tiles + i % self.tiles_per_batch


def _lnmod_kernel(h_ref, sh_ref, sc_ref, u_ref):
    u_ref[...] = (_ln(h_ref[...]) * (1.0 + sc_ref[...]) + sh_ref[...]).astype(u_ref.dtype)


def _lnmod(lay, h, mod3, shift_chunk, scale_chunk, out_dtype):
    row = pl.BlockSpec((TILE, D_MODEL), lambda i: (i, 0))
    return pl.pallas_call(
        _lnmod_kernel,
        out_shape=jax.ShapeDtypeStruct((lay.m, D_MODEL), out_dtype),
        grid=(lay.n_tiles,),
        in_specs=[row, lay.mod_spec(shift_chunk), lay.mod_spec(scale_chunk)],
        out_specs=row,
        compiler_params=_params(("parallel",), 2 * TILE * D_MODEL * 4),
        name="ln_modulate",
    )(h, mod3, mod3)


def _mm_kernel(a_ref, w_ref, o_ref):
    o_ref[...] = jnp.dot(a_ref[...], w_ref[...], preferred_element_type=F32).astype(o_ref.dtype)


def _matmul(a, w, layer, out_dtype=F32, tm=1024, tn=1024):
    m, k = a.shape
    n = w.shape[2]
    tn = min(tn, n)
    tm = math.gcd(m, tm)
    assert tm % TILE == 0 and n % tn == 0
    return pl.pallas_call(
        _mm_kernel,
        out_shape=jax.ShapeDtypeStruct((m, n), out_dtype),
        grid=(m // tm, n // tn),
        in_specs=[pl.BlockSpec((tm, k), lambda i, j: (i, 0)),
                  pl.BlockSpec((None, k, tn), lambda i, j: (layer, 0, j))],
        out_specs=pl.BlockSpec((tm, tn), lambda i, j: (i, j)),
        compiler_params=_params(("parallel", "parallel"), tm * k * 2 + k * tn * 2 + tm * tn * 4),
        name="matmul",
    )(a, w)


def _dt_kernel(a_ref, w_ref, b_ref, o_ref):
    v = jnp.dot(a_ref[...], w_ref[...], preferred_element_type=F32) + b_ref[...]
    o_ref[...] = jnp.maximum(v, 0.0) + jnp.log(1.0 + jnp.exp(-jnp.abs(v)))


def _dt_proj(a, w, layer, bias):
    m, k = a.shape
    n = w.shape[2]
    tm = math.gcd(m, 1024)
    return pl.pallas_call(
        _dt_kernel,
        out_shape=jax.ShapeDtypeStruct((m, n), F32),
        grid=(m // tm,),
        in_specs=[pl.BlockSpec((tm, k), lambda i: (i, 0)),
                  pl.BlockSpec((None, k, n), lambda i: (layer, 0, 0)),
                  pl.BlockSpec((1, n), lambda i: (0, 0))],
        out_specs=pl.BlockSpec((tm, n), lambda i: (i, 0)),
        compiler_params=_params(("parallel",), tm * k * 2 + k * n * 2 + tm * n * 4),
        name="dt_proj",
    )(a, w, bias)


IN_XBC = D_MODEL
IN_DT = IN_XBC + CONV_DIM
IN_Q = IN_DT + 2 * SSM_HEADS
IN_GATES = IN_Q + 3 * D_MODEL
PROJ_Z = 0
PROJ_GATES = PROJ_Z + D_MODEL
PROJ_Q = PROJ_GATES + 2 * D_MODEL
PROJ_XBC = PROJ_Q + 3 * D_MODEL
PROJ_COLS = PROJ_XBC + CONV_DIM


def _regroup_w_in(w_in):
    return jnp.concatenate([w_in[:, :, :IN_XBC], w_in[:, :, IN_GATES:], w_in[:, :, IN_Q:IN_GATES],
                            w_in[:, :, IN_XBC:IN_DT]], axis=-1).astype(BF16)


def _conv_kernel(x_ref, w_ref, b_ref, o_ref, *, ctx):
    x = x_ref[...]
    nt = x.shape[0]
    t = lax.broadcasted_iota(jnp.int32, x.shape, 0)
    lo = jnp.where(t < ctx, 0, ctx)
    hi = jnp.where(t < ctx, ctx, nt)
    pad = (CONV_WIDTH - 1) // 2
    acc = b_ref[...] + x * w_ref[pad:pad + 1, :]
    for k in range(CONV_WIDTH):
        d = k - pad
        if d == 0:
            continue
        shifted = pltpu.roll(x, (-d) % nt, axis=0)
        valid = (t + d >= lo) & (t + d < hi)
        acc = acc + jnp.where(valid, shifted, 0.0) * w_ref[k:k + 1, :]
    o_ref[...] = _silu(acc)


def _conv_silu(lay, proj, conv_w, conv_b):
    tc = 512
    off = PROJ_XBC // tc
    return pl.pallas_call(
        functools.partial(_conv_kernel, ctx=lay.ctx),
        out_shape=jax.ShapeDtypeStruct((lay.m, CONV_DIM), F32),
        grid=(lay.batch, CONV_DIM // tc),
        in_specs=[pl.BlockSpec((lay.nt, tc), lambda b, j: (b, off + j)),
                  pl.BlockSpec((CONV_WIDTH, tc), lambda b, j: (0, j)),
                  pl.BlockSpec((1, tc), lambda b, j: (0, j))],
        out_specs=pl.BlockSpec((lay.nt, tc), lambda b, j: (b, j)),
        compiler_params=_params(("parallel", "parallel"), 6 * lay.nt * tc * 4),
        name="conv_silu",
    )(proj, conv_w, conv_b.reshape(1, CONV_DIM))


def _ssd_direction(x_ref, b_ref, c_ref, dt_ref, alog_ref, st_ref, y_ref, *, lane_off, upper):
    L = SSD_CHUNK
    r = lax.broadcasted_iota(jnp.int32, (L, L), 0)
    s = lax.broadcasted_iota(jnp.int32, (L, L), 1)
    tri = (s >= r) if upper else (s <= r)
    x = x_ref[...]
    bm = b_ref[...]
    cm = c_ref[...].astype(BF16)
    dt = dt_ref[...]
    adt = dt * (-jnp.exp(alog_ref[...]))
    cs = jnp.dot(tri.astype(F32), adt, precision=HIGHEST, preferred_element_type=F32)
    cs_t = cs.T
    a_tot = jnp.sum(adt, axis=0, keepdims=True)
    cb = lax.dot_general(cm, bm.astype(BF16), (((1,), (1,)), ((), ())), preferred_element_type=F32)
    y_off = jnp.dot(cm, st_ref[...].astype(BF16), preferred_element_type=F32)
    bm_t = bm.T.astype(BF16)
    first_half = lax.broadcasted_iota(jnp.int32, (L, LANES), 1) < SSM_HEAD_DIM
    first_half_row = first_half[0:1, :]
    for pair in range(HEADS_PER_GROUP // 2):
        la = lane_off + 2 * pair
        lb = la + 1
        cols = slice(pair * LANES, (pair + 1) * LANES)
        col_a = jnp.broadcast_to(cs[:, la:la + 1], (L, LANES))
        col_b = jnp.broadcast_to(cs[:, lb:lb + 1], (L, LANES))
        cs_pair = jnp.where(first_half, col_a, col_b)
        dt_pair = jnp.where(first_half, dt[:, la:la + 1], dt[:, lb:lb + 1])
        tot_pair = jnp.where(first_half_row, a_tot[:, la:la + 1], a_tot[:, lb:lb + 1])
        xdt = x[:, cols] * dt_pair
        y_pair = y_off[:, cols] * jnp.exp(cs_pair)
        for col, lane, keep in ((col_a, la, first_half), (col_b, lb, jnp.logical_not(first_half))):
            seg = col - cs_t[lane:lane + 1, :]
            decay = jnp.where(tri, jnp.exp(seg), 0.0)
            mh = (cb * decay).astype(BF16)
            y_pair = y_pair + jnp.dot(mh, jnp.where(keep, xdt, 0.0).astype(BF16), preferred_element_type=F32)
        y_ref[:, cols] = y_pair
        z = (xdt * jnp.exp(tot_pair - cs_pair)).astype(BF16)
        st_ref[:, cols] = st_ref[:, cols] * jnp.exp(tot_pair) + jnp.dot(bm_t, z, preferred_element_type=F32)


def _ssd_kernel(xf, bf, cf, dtf, xr, br, cr, dtr, alog_ref, yf_ref, yr_ref, stf_ref, str_ref):
    @pl.when(pl.program_id(2) == 0)
    def _():
        stf_ref[...] = jnp.zeros_like(stf_ref)
        str_ref[...] = jnp.zeros_like(str_ref)

    for g in range(SSD_GROUP_BLOCK):
        wide = (slice(None), slice(g * GROUP_WIDTH, (g + 1) * GROUP_WIDTH))
        lanes = (slice(None), slice(g * LANES, (g + 1) * LANES))
        alog = alog_ref.at[lanes]
        _ssd_direction(xf.at[wide], bf.at[lanes], cf.at[lanes], dtf.at[lanes], alog, stf_ref.at[wide], yf_ref.at[wide],
                       lane_off=0, upper=False)
        _ssd_direction(xr.at[wide], br.at[lanes], cr.at[lanes], dtr.at[lanes], alog, str_ref.at[wide], yr_ref.at[wide],
                       lane_off=HEADS_PER_GROUP, upper=True)


def _ssd(lay, xbc, dt, a_log_g):
    L = SSD_CHUNK
    nc = lay.nt // L
    ctx_chunks = lay.ctx // L
    gb = SSD_GROUP_BLOCK
    b_col = D_MODEL // (gb * SSM_STATE)
    c_col = b_col + SSM_GROUPS // gb

    def fwd(t):
        return t

    def rev(t):
        return jnp.where(t < ctx_chunks, ctx_chunks - 1 - t, nc - 1 - (t - ctx_chunks))

    def specs(order):
        return [pl.BlockSpec((L, gb * GROUP_WIDTH), lambda b, g, t: (b * nc + order(t), g)),
                pl.BlockSpec((L, gb * SSM_STATE), lambda b, g, t: (b * nc + order(t), b_col + g)),
                pl.BlockSpec((L, gb * SSM_STATE), lambda b, g, t: (b * nc + order(t), c_col + g)),
                pl.BlockSpec((L, gb * LANES), lambda b, g, t: (b * nc + order(t), g))]

    small = pl.BlockSpec((1, gb * LANES), lambda b, g, t: (0, g))
    y_shape = jax.ShapeDtypeStruct((lay.m, D_MODEL), F32)
    state = pltpu.VMEM((SSM_STATE, gb * GROUP_WIDTH), F32)
    return pl.pallas_call(
        _ssd_kernel,
        out_shape=(y_shape, y_shape),
        grid=(lay.batch, SSM_GROUPS // gb, nc),
        in_specs=specs(fwd) + specs(rev) + [small],
        out_specs=(pl.BlockSpec((L, gb * GROUP_WIDTH), lambda b, g, t: (b * nc + fwd(t), g)),
                   pl.BlockSpec((L, gb * GROUP_WIDTH), lambda b, g, t: (b * nc + rev(t), g))),
        scratch_shapes=[state, state],
        compiler_params=_params(("parallel", "parallel", "arbitrary"), 8 * L * gb * GROUP_WIDTH * 4),
        name="ssd_scan",
    )(xbc, xbc, xbc, dt, xbc, xbc, xbc, dt, a_log_g)


def _ssd_out_kernel(yf_ref, yr_ref, x_ref, z_ref, d_ref, g_ref, o_ref):
    y = d_ref[...] * x_ref[...] + yf_ref[...] + yr_ref[...]
    yz = y * _silu(z_ref[...])
    for g in range(SSM_GROUPS):
        cols = slice(g * GROUP_WIDTH, (g + 1) * GROUP_WIDTH)
        v = yz[:, cols]
        inv = lax.rsqrt(jnp.mean(v * v, axis=-1, keepdims=True) + RMS_EPS)
        o_ref[:, cols] = (v * inv * g_ref[:, cols]).astype(o_ref.dtype)


def _ssd_out(lay, y_f, y_r, xbc, proj, d_exp, norm_g):
    src = lay.full_row_spec()
    vec = pl.BlockSpec((1, D_MODEL), lambda i: (0, 0))
    return pl.pallas_call(
        _ssd_out_kernel,
        out_shape=jax.ShapeDtypeStruct((lay.m, D_MODEL), BF16),
        grid=(lay.n_tiles,),
        in_specs=[src, src, src, src, vec, vec],
        out_specs=pl.BlockSpec((TILE, D_MODEL), lambda i: (i, 0)),
        compiler_params=_params(("parallel",), 6 * TILE * D_MODEL * 4),
        name="ssd_gated_norm",
    )(y_f, y_r, xbc, proj, d_exp, norm_g)


def _rope_tables(seq):
    half = ATT_HEAD_DIM // 2
    nf = half // 2
    inv = ROPE_BASE ** (-(jnp.arange(nf, dtype=F32) / nf))
    rows = seq // GRID_W
    row = jnp.repeat(jnp.arange(rows, dtype=F32), GRID_W)
    col = jnp.tile(jnp.arange(GRID_W, dtype=F32), rows)
    lane = jnp.arange(2 * ATT_HEAD_DIM)
    freq = inv[lane % nf]
    pos = jnp.where((lane % ATT_HEAD_DIM) < half, row[:, None], col[:, None])
    ang = pos * freq[None, :]
    low = (lane % half) < nf
    sin = jnp.sin(ang)
    return jnp.cos(ang), jnp.where(low, -sin, 0.0), jnp.where(low, 0.0, sin)


def _rope(t, cos, sin_lo, sin_hi):
    nf = ATT_HEAD_DIM // 4
    return t * cos + pltpu.roll(t, LANES - nf, axis=1) * sin_lo + pltpu.roll(t, nf, axis=1) * sin_hi


def _dot_nt(a, b):
    return lax.dot_general(a, b, (((1,), (1,)), ((), ())), preferred_element_type=F32)


def _attn_kernel(q_ref, k_ref, v_ref, cos_ref, slo_ref, shi_ref, lam_ref, g_ref, o_ref,
                 krot_ref, kctx_ref, vb_ref, *, ctx, ctx_tiles, lam_init):
    qi = pl.program_id(2)
    heads = [slice(hh * LANES, (hh + 1) * LANES) for hh in range(ATT_HEAD_BLOCK)]

    @pl.when(qi == 0)
    def _():
        for cols in heads:
            krot_ref[:, cols] = _rope(k_ref[ctx:, cols], cos_ref[...], slo_ref[...], shi_ref[...]).astype(BF16)
        kctx_ref[...] = k_ref[:ctx, :].astype(BF16)
        vb_ref[...] = v_ref[...].astype(BF16)

    lp = lam_ref[...]
    lam = (jnp.exp(jnp.sum(lp[0:1] * lp[1:2], axis=-1, keepdims=True))
           - jnp.exp(jnp.sum(lp[2:3] * lp[3:4], axis=-1, keepdims=True)) + lam_init)
    first_map = lax.broadcasted_iota(jnp.int32, (TILE, LANES), 1) < ATT_HEAD_DIM

    def split(q):
        return jnp.where(first_map, q, 0.0).astype(BF16), jnp.where(first_map, 0.0, q).astype(BF16)

    def softmax_pv(queries, keys, values):
        scores = [_dot_nt(q, k) for q, k in zip(queries, keys)]
        mx = functools.reduce(jnp.maximum, [jnp.max(s, axis=-1, keepdims=True) for s in scores])
        e = [jnp.exp2(s - mx) for s in scores]
        total = sum(jnp.sum(x, axis=-1, keepdims=True) for x in e)
        pv = sum(jnp.dot(x.astype(BF16), v, preferred_element_type=F32) for x, v in zip(e, values))
        return pv * (1.0 / total)

    def attend(cols, queries, keys, values):
        o = softmax_pv(queries[0], keys, values) - lam * softmax_pv(queries[1], keys, values)
        inv = lax.rsqrt(jnp.mean(o * o, axis=-1, keepdims=True) + RMS_EPS)
        o_ref[:, cols] = (o * inv * g_ref[...] * (1.0 - lam_init)).astype(o_ref.dtype)

    @pl.when(qi < ctx_tiles)
    def _():
        for cols in heads:
            qn = split(q_ref[:, cols] * ATT_Q_SCALE)
            attend(cols, [[qn[0]], [qn[1]]], [kctx_ref[:, cols]], [vb_ref[:ctx, cols]])

    @pl.when(qi >= ctx_tiles)
    def _():
        start = pl.multiple_of((qi - ctx_tiles) * TILE, TILE)
        rows = pl.ds(start, TILE)
        for cols in heads:
            q = q_ref[:, cols] * ATT_Q_SCALE
            qn = split(q)
            qr = split(_rope(q, cos_ref[rows, :], slo_ref[rows, :], shi_ref[rows, :]))
            attend(cols, [[qn[0], qr[0]], [qn[1], qr[1]]], [kctx_ref[:, cols], krot_ref[:, cols]],
                   [vb_ref[:ctx, cols], vb_ref[ctx:, cols]])


def _attention(lay, qkv, tables, lam_p, subln_g, lam_init):
    width = ATT_HEAD_BLOCK * LANES
    hb = D_MODEL // width
    q0 = PROJ_Q // width
    tab = pl.BlockSpec((lay.seq, LANES), lambda b, h, i: (0, 0))
    kv_bytes = lay.nt * width * 4
    return pl.pallas_call(
        functools.partial(_attn_kernel, ctx=lay.ctx, ctx_tiles=lay.ctx_tiles, lam_init=lam_init),
        out_shape=jax.ShapeDtypeStruct((lay.m, D_MODEL), BF16),
        grid=(lay.batch, ATT_HEADS // ATT_HEAD_BLOCK, lay.tiles_per_batch),
        in_specs=[pl.BlockSpec((TILE, width), lambda b, h, i: (b * lay.tiles_per_batch + i, q0 + h)),
                  pl.BlockSpec((lay.nt, width), lambda b, h, i: (b, q0 + hb + h)),
                  pl.BlockSpec((lay.nt, width), lambda b, h, i: (b, q0 + 2 * hb + h)),
                  tab, tab, tab,
                  pl.BlockSpec((8, ATT_HEAD_DIM), lambda b, h, i: (0, 0)),
                  pl.BlockSpec((1, LANES), lambda b, h, i: (0, 0))],
        out_specs=pl.BlockSpec((TILE, width), lambda b, h, i: (b * lay.tiles_per_batch + i, h)),
        scratch_shapes=[pltpu.VMEM((lay.seq, width), BF16), pltpu.VMEM((lay.ctx, width), BF16),
                        pltpu.VMEM((lay.nt, width), BF16)],
        compiler_params=_params(("parallel", "parallel", "arbitrary"),
                                2 * kv_bytes + 3 * lay.seq * LANES * 4 + 4 * TILE * lay.nt * 4),
        name="diff_attention",
    )(qkv, qkv, qkv, *tables, lam_p, subln_g)


def _route(logits):
    lane_i = lax.broadcasted_iota(jnp.int32, logits.shape, 1)
    lane = lane_i.astype(F32)
    none = float(LANES)
    neg = -jnp.inf

    def softmax(mask):
        v = jnp.where(mask, logits, neg)
        e = jnp.exp(v - jnp.max(v, axis=-1, keepdims=True))
        return e / jnp.sum(e, axis=-1, keepdims=True)

    def top(p, mask):
        best = jnp.max(jnp.where(mask, p, -1.0), axis=-1, keepdims=True)
        idx = jnp.min(jnp.where(mask & (p == best), lane, none), axis=-1, keepdims=True)
        return best, idx

    is_group = lane_i < MOE_GROUPS
    p_grp, grp = top(softmax(is_group), is_group)
    first = MOE_GROUPS + grp * EXPERTS_PER_GROUP
    in_grp = (lane >= first) & (lane < first + EXPERTS_PER_GROUP)
    pe = softmax(in_grp)
    p1, i1 = top(pe, in_grp)
    rest = in_grp & (lane != i1)
    p2, i2 = top(pe, rest)
    denom = p1 + p2
    w1 = p_grp * p1 / denom
    w2 = p_grp * p2 / denom
    ids = jnp.where(lane_i == 0, i1 - MOE_GROUPS, jnp.where(lane_i == 1, i2 - MOE_GROUPS, 0.0))
    return ids, jnp.where(lane_i == 0, w1, jnp.where(lane_i == 1, w2, 0.0))


def _split_bf16(x):
    hi = x.astype(BF16)
    return hi, (x - hi.astype(F32)).astype(BF16)


def _mixer_out_kernel(ys_ref, ya_ref, ws_ref, wa_ref, wo_ref, gs_ref, ga_ref, h_ref, gate_ref, lg_ref, lb_ref,
                      sh_ref, sc_ref, rh_ref, rl_ref, rb_ref, h_out, u_out, ids_out, wts_out, *, alpha):
    ps = jnp.dot(ys_ref[...], ws_ref[...], preferred_element_type=F32)
    pa = jnp.dot(ya_ref[...], wa_ref[...], preferred_element_type=F32)
    m = (jax.nn.sigmoid(gs_ref[...]) * ps + jax.nn.sigmoid(ga_ref[...]) * pa).astype(BF16)
    o = jnp.dot(m, wo_ref[...], preferred_element_type=F32)
    h1 = _ln(alpha * h_ref[...] + gate_ref[...] * o) * lg_ref[...] + lb_ref[...]
    h_out[...] = h1
    u = _ln(h1) * (1.0 + sc_ref[...]) + sh_ref[...]
    u_hi, u_lo = _split_bf16(u)
    u_out[...] = u_hi.reshape(u_out.shape)
    logits = (jnp.dot(u_hi, rh_ref[...], preferred_element_type=F32)
              + (jnp.dot(u_hi, rl_ref[...], preferred_element_type=F32)
                 + jnp.dot(u_lo, rh_ref[...], preferred_element_type=F32))) + rb_ref[...]
    ids, wts = _route(logits)
    ids_out[...] = ids.astype(jnp.int32)
    wts_out[...] = wts


def _mixer_out(lay, layer, y_s, y_a, w_s, w_a, w_o, proj, h, mod3, ln_g, ln_b, r_hi, r_lo, r_b, alpha):
    row = pl.BlockSpec((TILE, D_MODEL), lambda i: (i, 0))
    vec = pl.BlockSpec((1, D_MODEL), lambda i: (0, 0))
    weight = pl.BlockSpec((None, D_MODEL, D_MODEL), lambda i: (layer, 0, 0), pipeline_mode=pl.Buffered(1))
    rw = pl.BlockSpec((D_MODEL, LANES), lambda i: (0, 0))
    small = pl.BlockSpec((TILE, LANES), lambda i: (i, 0))
    g0 = PROJ_GATES // D_MODEL
    return pl.pallas_call(
        functools.partial(_mixer_out_kernel, alpha=alpha),
        out_shape=(jax.ShapeDtypeStruct((lay.m, D_MODEL), F32), jax.ShapeDtypeStruct((lay.m, ROW_TILES, LANES), BF16),
                   jax.ShapeDtypeStruct((lay.m, LANES), jnp.int32), jax.ShapeDtypeStruct((lay.m, LANES), F32)),
        grid=(lay.n_tiles,),
        in_specs=[row, lay.full_row_spec(), weight, weight, weight,
                  lay.full_row_spec(g0), lay.full_row_spec(g0 + 1),
                  lay.full_row_spec(), lay.mod_spec(2), vec, vec, lay.mod_spec(3), lay.mod_spec(4),
                  rw, rw, pl.BlockSpec((1, LANES), lambda i: (0, 0))],
        out_specs=(row, pl.BlockSpec((TILE, ROW_TILES, LANES), lambda i: (i, 0, 0)), small, small),
        compiler_params=pltpu.CompilerParams(
            dimension_semantics=("parallel",),
            vmem_limit_bytes=int(min(VMEM_CAP, 3 * D_MODEL * D_MODEL * 2 + 2 * 7 * TILE * D_MODEL * 4 + VMEM_SLACK))),
        name="mixer_out",
    )(y_s, y_a, w_s, w_a, w_o, proj, proj, h, mod3, ln_g, ln_b, mod3, mod3, r_hi, r_lo, r_b)


def _dispatch(ids, n_tok):
    e_flat = ids[:, :MOE_TOP_K].reshape(-1)
    n_assign = n_tok * MOE_TOP_K
    onehot = (e_flat[:, None] == jnp.arange(N_EXPERTS, dtype=jnp.int32)[None, :]).astype(jnp.int32)
    csum = jnp.cumsum(onehot, axis=0)
    rank = jnp.sum(csum * onehot, axis=1) - 1
    counts = csum[-1]
    padded = (counts + MOE_BLOCK - 1) // MOE_BLOCK * MOE_BLOCK
    pend = jnp.cumsum(padded)
    pstarts = pend - padded
    dest = (pstarts[e_flat] + rank).astype(jnp.int32)
    n_blocks = n_assign // MOE_BLOCK + N_EXPERTS
    tok_flat = jnp.arange(n_assign, dtype=jnp.int32) // MOE_TOP_K
    slot_tok = jnp.zeros((n_blocks * MOE_BLOCK,), jnp.int32).at[dest].set(tok_flat)
    block_start = jnp.arange(n_blocks, dtype=jnp.int32) * MOE_BLOCK
    block_exp = jnp.minimum(jnp.sum(block_start[:, None] >= pend[None, :], axis=1), N_EXPERTS - 1).astype(jnp.int32)
    n_used = (pend[-1] // MOE_BLOCK).astype(jnp.int32).reshape(1)
    return dest, slot_tok, block_exp, n_used, n_blocks


def _row_copy(src_hbm, src_row, dst, dst_row, sem):
    return pltpu.make_async_copy(src_hbm.at[pl.ds(src_row, 1)], dst.at[pl.ds(dst_row, 1)], sem)


def _gather_rows(src_hbm, idx_ref, base, dst, sem, n_rows, start, dst_row=lambda r: r):
    for r in range(n_rows):
        cp = _row_copy(src_hbm, idx_ref[base + r], dst, dst_row(r), sem)
        if start:
            cp.start()
        else:
            cp.wait()


def _expert_kernel(bexp_ref, stok_ref, nused_ref, u_hbm, wg_ref, wu_ref, wd_ref, o_ref, xbuf, sem):
    i = pl.program_id(0)
    n = nused_ref[0]
    slot = i % 2

    def gather(blk, s, start):
        _gather_rows(u_hbm, stok_ref, blk * MOE_BLOCK, xbuf.at[s], sem.at[s], MOE_BLOCK, start)

    @pl.when(i == 0)
    def _():
        gather(0, 0, True)

    @pl.when(i < n)
    def _():
        gather(i, slot, False)
        gather(i + 1, 1 - slot, True)
        x = xbuf[slot].reshape(MOE_BLOCK, D_MODEL)
        hid = _silu(jnp.dot(x, wg_ref[...], preferred_element_type=F32)) * jnp.dot(x, wu_ref[...], preferred_element_type=F32)
        y = jnp.dot(hid.astype(BF16), wd_ref[...], preferred_element_type=F32)
        o_ref[...] = y.reshape(o_ref.shape)

    @pl.when(i == n - 1)
    def _():
        gather(i + 1, 1 - slot, False)

    @pl.when(i >= n)
    def _():
        o_ref[...] = jnp.zeros_like(o_ref)


def _experts(layer, u, block_exp, slot_tok, n_used, n_blocks, w_gate, w_up, w_down):
    wspec_in = pl.BlockSpec((None, None, D_MODEL, EXPERT_FF), lambda i, be, st, nu: (layer, be[i], 0, 0))
    grid_spec = pltpu.PrefetchScalarGridSpec(
        num_scalar_prefetch=3,
        grid=(n_blocks,),
        in_specs=[pl.BlockSpec(memory_space=pl.ANY), wspec_in, wspec_in,
                  pl.BlockSpec((None, None, EXPERT_FF, D_MODEL), lambda i, be, st, nu: (layer, be[i], 0, 0))],
        out_specs=pl.BlockSpec((MOE_BLOCK, ROW_TILES, LANES), lambda i, be, st, nu: (i, 0, 0)),
        scratch_shapes=[pltpu.VMEM((2, MOE_BLOCK, ROW_TILES, LANES), BF16), pltpu.SemaphoreType.DMA((2,))],
    )
    return pl.pallas_call(
        _expert_kernel,
        out_shape=jax.ShapeDtypeStruct((n_blocks * MOE_BLOCK, ROW_TILES, LANES), F32),
        grid_spec=grid_spec,
        compiler_params=_params(("arbitrary",), 3 * D_MODEL * EXPERT_FF * 2 + 2 * MOE_BLOCK * D_MODEL * 4),
        name="moe_experts",
    )(block_exp, slot_tok, n_used, u, w_gate, w_up, w_down)


def _combine_kernel(dest_ref, y_hbm, wts_ref, h_ref, gate_ref, lg_ref, lb_ref, sh_ref, sc_ref, h_out, u_out,
                    ybuf, sem, *, alpha, n_tiles):
    i = pl.program_id(0)
    slot = i % 2
    rows = TILE * MOE_TOP_K

    def gather(tile, s, start):
        _gather_rows(y_hbm, dest_ref, tile * rows, ybuf.at[s], sem.at[s], rows, start,
                     dst_row=lambda r: (r % MOE_TOP_K) * TILE + r // MOE_TOP_K)

    @pl.when(i == 0)
    def _():
        gather(0, 0, True)

    ahead = jnp.minimum(i + 1, n_tiles - 1)
    gather(i, slot, False)
    gather(ahead, 1 - slot, True)
    wts = wts_ref[...]
    y = (wts[:, 0:1] * ybuf[slot, 0:TILE].reshape(TILE, D_MODEL)
         + wts[:, 1:2] * ybuf[slot, TILE:2 * TILE].reshape(TILE, D_MODEL))
    h2 = _ln(alpha * h_ref[...] + gate_ref[...] * y) * lg_ref[...] + lb_ref[...]
    h_out[...] = h2
    u_out[...] = (_ln(h2) * (1.0 + sc_ref[...]) + sh_ref[...]).astype(u_out.dtype)

    @pl.when(i == n_tiles - 1)
    def _():
        gather(ahead, 1 - slot, False)


def _combine(lay, dest, y_slots, wts, h, mod3, mod3_next, ln_g, ln_b, alpha):
    row = pl.BlockSpec((TILE, D_MODEL), lambda i, d: (i, 0))
    vec = pl.BlockSpec((1, D_MODEL), lambda i, d: (0, 0))
    grid_spec = pltpu.PrefetchScalarGridSpec(
        num_scalar_prefetch=1,
        grid=(lay.n_tiles,),
        in_specs=[pl.BlockSpec(memory_space=pl.ANY), pl.BlockSpec((TILE, LANES), lambda i, d: (i, 0)), row,
                  lay.mod_spec(5), vec, vec, lay.mod_spec(0), lay.mod_spec(1)],
        out_specs=(row, row),
        scratch_shapes=[pltpu.VMEM((2, MOE_TOP_K * TILE, ROW_TILES, LANES), F32), pltpu.SemaphoreType.DMA((2,))],
    )
    return pl.pallas_call(
        functools.partial(_combine_kernel, alpha=alpha, n_tiles=lay.n_tiles),
        out_shape=(jax.ShapeDtypeStruct((lay.m, D_MODEL), F32), jax.ShapeDtypeStruct((lay.m, D_MODEL), BF16)),
        grid_spec=grid_spec,
        compiler_params=_params(("arbitrary",), 5 * TILE * D_MODEL * 4),
        name="moe_combine_residual",
    )(dest, y_slots, wts, h, mod3, ln_g, ln_b, mod3_next, mod3_next)


def _group_lanes(v):
    per_group = v.reshape(2, SSM_GROUPS, HEADS_PER_GROUP).transpose(1, 0, 2).reshape(SSM_GROUPS, 2 * HEADS_PER_GROUP)
    return jnp.pad(per_group, ((0, 0), (0, LANES - 2 * HEADS_PER_GROUP))).reshape(1, SSM_GROUPS * LANES)


def kernel(x, c, ctx, c_ctx, w_ada, b_ada, w_in, conv_w, conv_b, ssm_dt_bias, ssm_a_log, ssm_d, ssm_norm_g, lam_q1, lam_k1, lam_q2, lam_k2, attn_subln_g, w_br_ssm, w_br_att, w_out, ln1_g, ln1_b, w_router_group, b_router_group, w_router_expert, b_router_expert, w_exp_gate, w_exp_up, w_exp_down, ln2_g, ln2_b):
    batch, seq, d = x.shape
    depth = w_ada.shape[0]
    assert d == D_MODEL and batch < MOD_ROWS
    lay = _Layout(batch, seq, ctx.shape[1])
    alpha = (2.0 * depth) ** 0.25

    c_all = jnp.zeros((MOD_ROWS, d), F32).at[:batch].set(c).at[batch].set(c_ctx)
    mod = _adaln(c_all, w_ada, b_ada).reshape(depth, MOD_ROWS, 1, 6 * d)
    tables = _rope_tables(seq)

    w_proj = _regroup_w_in(w_in)
    w_bs, w_ba, w_o = w_br_ssm.astype(BF16), w_br_att.astype(BF16), w_out.astype(BF16)
    w_eg, w_eu, w_ed = w_exp_gate.astype(BF16), w_exp_up.astype(BF16), w_exp_down.astype(BF16)
    w_dt = w_in[:, :, IN_DT:IN_Q].reshape(depth, d, 2, SSM_GROUPS, HEADS_PER_GROUP).transpose(0, 1, 3, 2, 4)
    w_dt = jnp.pad(w_dt.reshape(depth, d, SSM_GROUPS, 2 * HEADS_PER_GROUP),
                   ((0, 0), (0, 0), (0, 0), (0, LANES - 2 * HEADS_PER_GROUP)))
    w_dt = w_dt.reshape(depth, d, SSM_GROUPS * LANES).astype(BF16)

    h = jnp.concatenate([ctx, x], axis=1).reshape(lay.m, d)
    u = _lnmod(lay, h, mod[0], 0, 1, BF16)
    for l in range(depth):
        lam_init = 0.8 - 0.6 * math.exp(-0.3 * l)
        proj = _matmul(u, w_proj, l)
        dt = _dt_proj(u, w_dt, l, _group_lanes(ssm_dt_bias[l]))

        xbc = _conv_silu(lay, proj, conv_w[l], conv_b[l])
        y_f, y_r = _ssd(lay, xbc, dt, _group_lanes(ssm_a_log[l]))
        out_lay = _LatentLayout(lay) if l == depth - 1 else lay
        y_s = _ssd_out(out_lay, y_f, y_r, xbc, proj, jnp.repeat(ssm_d[l], SSM_HEAD_DIM).reshape(1, d),
                       ssm_norm_g[l].reshape(1, d))

        lam_p = jnp.zeros((8, ATT_HEAD_DIM), F32).at[0].set(lam_q1[l]).at[1].set(lam_k1[l]).at[2].set(lam_q2[l]).at[3].set(lam_k2[l])
        y_a = _attention(lay, proj, tables, lam_p, attn_subln_g[l].reshape(1, LANES), lam_init)

        w_r = jnp.zeros((d, LANES), F32).at[:, :MOE_GROUPS].set(w_router_group[l]).at[:, MOE_GROUPS:MOE_GROUPS + N_EXPERTS].set(w_router_expert[l])
        b_r = jnp.zeros((1, LANES), F32).at[0, :MOE_GROUPS].set(b_router_group[l]).at[0, MOE_GROUPS:MOE_GROUPS + N_EXPERTS].set(b_router_expert[l])
        r_hi, r_lo = _split_bf16(w_r)
        h, u2, ids, wts = _mixer_out(out_lay, l, y_s, y_a, w_bs, w_ba, w_o, proj, h, mod[l], ln1_g[l].reshape(1, d),
                                     ln1_b[l].reshape(1, d), r_hi, r_lo, b_r, alpha)
        dest, slot_tok, block_exp, n_used, n_blocks = _dispatch(ids, out_lay.m)
        y_slots = _experts(l, u2, block_exp, slot_tok, n_used, n_blocks, w_eg, w_eu, w_ed)
        h, u = _combine(out_lay, dest, y_slots, wts, h, mod[l], mod[min(l + 1, depth - 1)],
                        ln2_g[l].reshape(1, d), ln2_b[l].reshape(1, d), alpha)
    return h.reshape(batch, seq, d)
```

```python
import functools
import math

import jax
import jax.numpy as jnp
from jax import lax
from jax.experimental import pallas as pl
from jax.experimental.pallas import tpu as pltpu

F32 = jnp.float32
BF16 = jnp.bfloat16
HIGHEST = lax.Precision.HIGHEST

D_MODEL = 2048
GRID_W = 64

SSM_HEAD_DIM = 64
SSM_HEADS = D_MODEL // SSM_HEAD_DIM
SSM_GROUPS = 4
SSM_STATE = 128
HEADS_PER_GROUP = SSM_HEADS // SSM_GROUPS
GROUP_WIDTH = HEADS_PER_GROUP * SSM_HEAD_DIM
CONV_WIDTH = 5
CONV_DIM = D_MODEL + 2 * SSM_GROUPS * SSM_STATE
SSD_CHUNK = 128
SSD_GROUP_BLOCK = 4

ATT_HEAD_DIM = 64
ATT_HEADS = D_MODEL // (2 * ATT_HEAD_DIM)
ROPE_BASE = 10000.0
ATT_HEAD_BLOCK = 4
ATT_Q_SCALE = ATT_HEAD_DIM ** -0.5 * math.log2(math.e)

MOE_GROUPS = 4
EXPERTS_PER_GROUP = 8
N_EXPERTS = MOE_GROUPS * EXPERTS_PER_GROUP
MOE_TOP_K = 2
EXPERT_FF = D_MODEL // 2

LN_EPS = 1e-6
RMS_EPS = 1e-5

LANES = 128
ROW_TILES = D_MODEL // LANES
MOD_ROWS = 16
TILE = 256
MOE_BLOCK = 128
VMEM_CAP = 60 * 1024 * 1024
VMEM_SLACK = 6 * 1024 * 1024


def _params(sem, block_bytes):
    return pltpu.CompilerParams(dimension_semantics=sem,
                                vmem_limit_bytes=int(min(VMEM_CAP, 2 * block_bytes + VMEM_SLACK)))


def _ln(x):
    mu = jnp.mean(x, axis=-1, keepdims=True)
    xc = x - mu
    var = jnp.mean(xc * xc, axis=-1, keepdims=True)
    return xc * lax.rsqrt(var + LN_EPS)


def _silu(x):
    return x * jax.nn.sigmoid(x)


def _mod_kernel(c_ref, w_ref, b_ref, o_ref):
    sc = _silu(c_ref[...]).astype(BF16)
    o_ref[...] = jnp.dot(sc, w_ref[...].astype(BF16), preferred_element_type=F32) + b_ref[...]


def _adaln(c_all, w_ada, b_ada):
    depth, d, n = w_ada.shape
    tn = 1024
    return pl.pallas_call(
        _mod_kernel,
        out_shape=jax.ShapeDtypeStruct((depth, MOD_ROWS, n), F32),
        grid=(depth, n // tn),
        in_specs=[pl.BlockSpec((MOD_ROWS, d), lambda l, j: (0, 0)),
                  pl.BlockSpec((None, d, tn), lambda l, j: (l, 0, j)),
                  pl.BlockSpec((None, 1, tn), lambda l, j: (l, 0, j))],
        out_specs=pl.BlockSpec((None, MOD_ROWS, tn), lambda l, j: (l, 0, j)),
        compiler_params=_params(("parallel", "parallel"), d * tn * 4 + d * tn * 2),
        name="adaln_mod",
    )(c_all, w_ada, b_ada.reshape(depth, 1, n))


class _Layout:
    def __init__(self, batch, seq, ctx):
        assert ctx % TILE == 0 and seq % TILE == 0 and seq % GRID_W == 0
        self.batch, self.seq, self.ctx = batch, seq, ctx
        self.nt = ctx + seq
        self.m = batch * self.nt
        self.tiles_per_batch = self.nt // TILE
        self.ctx_tiles = ctx // TILE
        self.n_tiles = self.m // TILE

    def mod_row(self, i):
        return jnp.where(i % self.tiles_per_batch < self.ctx_tiles, self.batch, i // self.tiles_per_batch)

    def mod_spec(self, chunk):
        return pl.BlockSpec((None, 1, D_MODEL), lambda i, *_: (self.mod_row(i), 0, chunk))

    def full_tile(self, i):
        return i

    def full_row_spec(self, col_block=0):
        return pl.BlockSpec((TILE, D_MODEL), lambda i, *_: (self.full_tile(i), col_block))


class _LatentLayout(_Layout):
    def __init__(self, full):
        self.batch, self.seq, self.ctx = full.batch, full.seq, full.ctx
        self.m = full.batch * full.seq
        self.tiles_per_batch = full.seq // TILE
        self.n_tiles = self.m // TILE
        self._full = full

    def mod_row(self, i):
        return i // self.tiles_per_batch

    def full_tile(self, i):
        f = self._full
        return (i // self.tiles_per_batch) * f.tiles_per_batch + f.ctx_tiles + i % self.tiles_per_batch


def _lnmod_kernel(h_ref, sh_ref, sc_ref, u_ref):
    u_ref[...] = (_ln(h_ref[...]) * (1.0 + sc_ref[...]) + sh_ref[...]).astype(u_ref.dtype)


def _lnmod(lay, h, mod3, shift_chunk, scale_chunk, out_dtype):
    row = pl.BlockSpec((TILE, D_MODEL), lambda i: (i, 0))
    return pl.pallas_call(
        _lnmod_kernel,
        out_shape=jax.ShapeDtypeStruct((lay.m, D_MODEL), out_dtype),
        grid=(lay.n_tiles,),
        in_specs=[row, lay.mod_spec(shift_chunk), lay.mod_spec(scale_chunk)],
        out_specs=row,
        compiler_params=_params(("parallel",), 2 * TILE * D_MODEL * 4),
        name="ln_modulate",
    )(h, mod3, mod3)


def _mm_kernel(a_ref, w_ref, o_ref):
    o_ref[...] = jnp.dot(a_ref[...], w_ref[...], preferred_element_type=F32).astype(o_ref.dtype)


def _matmul(a, w, layer, out_dtype=F32, tm=1024, tn=1024):
    m, k = a.shape
    n = w.shape[2]
    tn = min(tn, n)
    tm = math.gcd(m, tm)
    assert tm % TILE == 0 and n % tn == 0
    return pl.pallas_call(
        _mm_kernel,
        out_shape=jax.ShapeDtypeStruct((m, n), out_dtype),
        grid=(m // tm, n // tn),
        in_specs=[pl.BlockSpec((tm, k), lambda i, j: (i, 0)),
                  pl.BlockSpec((None, k, tn), lambda i, j: (layer, 0, j))],
        out_specs=pl.BlockSpec((tm, tn), lambda i, j: (i, j)),
        compiler_params=_params(("parallel", "parallel"), tm * k * 2 + k * tn * 2 + tm * tn * 4),
        name="matmul",
    )(a, w)


def _dt_kernel(a_ref, w_ref, b_ref, o_ref):
    v = jnp.dot(a_ref[...], w_ref[...], preferred_element_type=F32) + b_ref[...]
    o_ref[...] = jnp.maximum(v, 0.0) + jnp.log(1.0 + jnp.exp(-jnp.abs(v)))


def _dt_proj(a, w, layer, bias):
    m, k = a.shape
    n = w.shape[2]
    tm = math.gcd(m, 1024)
    return pl.pallas_call(
        _dt_kernel,
        out_shape=jax.ShapeDtypeStruct((m, n), F32),
        grid=(m // tm,),
        in_specs=[pl.BlockSpec((tm, k), lambda i: (i, 0)),
                  pl.BlockSpec((None, k, n), lambda i: (layer, 0, 0)),
                  pl.BlockSpec((1, n), lambda i: (0, 0))],
        out_specs=pl.BlockSpec((tm, n), lambda i: (i, 0)),
        compiler_params=_params(("parallel",), tm * k * 2 + k * n * 2 + tm * n * 4),
        name="dt_proj",
    )(a, w, bias)


IN_XBC = D_MODEL
IN_DT = IN_XBC + CONV_DIM
IN_Q = IN_DT + 2 * SSM_HEADS
IN_GATES = IN_Q + 3 * D_MODEL
PROJ_Z = 0
PROJ_GATES = PROJ_Z + D_MODEL
PROJ_Q = PROJ_GATES + 2 * D_MODEL
PROJ_XBC = PROJ_Q + 3 * D_MODEL
PROJ_COLS = PROJ_XBC + CONV_DIM


def _regroup_w_in(w_in):
    return jnp.concatenate([w_in[:, :, :IN_XBC], w_in[:, :, IN_GATES:], w_in[:, :, IN_Q:IN_GATES],
                            w_in[:, :, IN_XBC:IN_DT]], axis=-1).astype(BF16)


def _conv_kernel(x_ref, w_ref, b_ref, o_ref, *, ctx):
    x = x_ref[...]
    nt = x.shape[0]
    t = lax.broadcasted_iota(jnp.int32, x.shape, 0)
    lo = jnp.where(t < ctx, 0, ctx)
    hi = jnp.where(t < ctx, ctx, nt)
    pad = (CONV_WIDTH - 1) // 2
    acc = b_ref[...] + x * w_ref[pad:pad + 1, :]
    for k in range(CONV_WIDTH):
        d = k - pad
        if d == 0:
            continue
        shifted = pltpu.roll(x, (-d) % nt, axis=0)
        valid = (t + d >= lo) & (t + d < hi)
        acc = acc + jnp.where(valid, shifted, 0.0) * w_ref[k:k + 1, :]
    o_ref[...] = _silu(acc)


def _conv_silu(lay, proj, conv_w, conv_b):
    tc = 512
    off = PROJ_XBC // tc
    return pl.pallas_call(
        functools.partial(_conv_kernel, ctx=lay.ctx),
        out_shape=jax.ShapeDtypeStruct((lay.m, CONV_DIM), F32),
        grid=(lay.batch, CONV_DIM // tc),
        in_specs=[pl.BlockSpec((lay.nt, tc), lambda b, j: (b, off + j)),
                  pl.BlockSpec((CONV_WIDTH, tc), lambda b, j: (0, j)),
                  pl.BlockSpec((1, tc), lambda b, j: (0, j))],
        out_specs=pl.BlockSpec((lay.nt, tc), lambda b, j: (b, j)),
        compiler_params=_params(("parallel", "parallel"), 6 * lay.nt * tc * 4),
        name="conv_silu",
    )(proj, conv_w, conv_b.reshape(1, CONV_DIM))


def _ssd_direction(x_ref, b_ref, c_ref, dt_ref, alog_ref, st_ref, y_ref, *, lane_off, upper):
    L = SSD_CHUNK
    r = lax.broadcasted_iota(jnp.int32, (L, L), 0)
    s = lax.broadcasted_iota(jnp.int32, (L, L), 1)
    tri = (s >= r) if upper else (s <= r)
    x = x_ref[...]
    bm = b_ref[...]
    cm = c_ref[...].astype(BF16)
    dt = dt_ref[...]
    adt = dt * (-jnp.exp(alog_ref[...]))
    cs = jnp.dot(tri.astype(F32), adt, precision=HIGHEST, preferred_element_type=F32)
    cs_t = cs.T
    a_tot = jnp.sum(adt, axis=0, keepdims=True)
    cb = lax.dot_general(cm, bm.astype(BF16), (((1,), (1,)), ((), ())), preferred_element_type=F32)
    y_off = jnp.dot(cm, st_ref[...].astype(BF16), preferred_element_type=F32)
    bm_t = bm.T
    dt_t = dt.T
    first_half = lax.broadcasted_iota(jnp.int32, (L, LANES), 1) < SSM_HEAD_DIM
    first_half_row = first_half[0:1, :]
    for pair in range(HEADS_PER_GROUP // 2):
        la = lane_off + 2 * pair
        lb = la + 1
        cols = slice(pair * LANES, (pair + 1) * LANES)
        col_a = jnp.broadcast_to(cs[:, la:la + 1], (L, LANES))
        col_b = jnp.broadcast_to(cs[:, lb:lb + 1], (L, LANES))
        cs_pair = jnp.where(first_half, col_a, col_b)
        tot_pair = jnp.where(first_half_row, a_tot[:, la:la + 1], a_tot[:, lb:lb + 1])
        xp = x[:, cols]
        y_pair = y_off[:, cols] * jnp.exp(cs_pair)
        zx = xp * jnp.exp(tot_pair - cs_pair)
        st_new = st_ref[:, cols] * jnp.exp(tot_pair)
        for col, lane, keep in ((col_a, la, first_half), (col_b, lb, jnp.logical_not(first_half))):
            dt_row = dt_t[lane:lane + 1, :]
            seg = col - cs_t[lane:lane + 1, :]
            decay = jnp.where(tri, jnp.exp(seg), 0.0) * dt_row
            mh = (cb * decay).astype(BF16)
            y_pair = y_pair + jnp.dot(mh, jnp.where(keep, xp, 0.0).astype(BF16), preferred_element_type=F32)
            st_new = st_new + jnp.dot((bm_t * dt_row).astype(BF16), jnp.where(keep, zx, 0.0).astype(BF16),
                                      preferred_element_type=F32)
        y_ref[:, cols] = y_pair
        st_ref[:, cols] = st_new


def _ssd_kernel(xf, bf, cf, dtf, xr, br, cr, dtr, alog_ref, yf_ref, yr_ref, stf_ref, str_ref):
    @pl.when(pl.program_id(2) == 0)
    def _():
        stf_ref[...] = jnp.zeros_like(stf_ref)
        str_ref[...] = jnp.zeros_like(str_ref)

    for g in range(SSD_GROUP_BLOCK):
        wide = (slice(None), slice(g * GROUP_WIDTH, (g + 1) * GROUP_WIDTH))
        lanes = (slice(None), slice(g * LANES, (g + 1) * LANES))
        alog = alog_ref.at[lanes]
        _ssd_direction(xf.at[wide], bf.at[lanes], cf.at[lanes], dtf.at[lanes], alog, stf_ref.at[wide], yf_ref.at[wide],
                       lane_off=0, upper=False)
        _ssd_direction(xr.at[wide], br.at[lanes], cr.at[lanes], dtr.at[lanes], alog, str_ref.at[wide], yr_ref.at[wide],
                       lane_off=HEADS_PER_GROUP, upper=True)


def _ssd(lay, xbc, dt, a_log_g):
    L = SSD_CHUNK
    nc = lay.nt // L
    ctx_chunks = lay.ctx // L
    gb = SSD_GROUP_BLOCK
    b_col = D_MODEL // (gb * SSM_STATE)
    c_col = b_col + SSM_GROUPS // gb

    def fwd(t):
        return t

    def rev(t):
        return jnp.where(t < ctx_chunks, ctx_chunks - 1 - t, nc - 1 - (t - ctx_chunks))

    def specs(order):
        return [pl.BlockSpec((L, gb * GROUP_WIDTH), lambda b, g, t: (b * nc + order(t), g)),
                pl.BlockSpec((L, gb * SSM_STATE), lambda b, g, t: (b * nc + order(t), b_col + g)),
                pl.BlockSpec((L, gb * SSM_STATE), lambda b, g, t: (b * nc + order(t), c_col + g)),
                pl.BlockSpec((L, gb * LANES), lambda b, g, t: (b * nc + order(t), g))]

    small = pl.BlockSpec((1, gb * LANES), lambda b, g, t: (0, g))
    y_shape = jax.ShapeDtypeStruct((lay.m, D_MODEL), F32)
    state = pltpu.VMEM((SSM_STATE, gb * GROUP_WIDTH), F32)
    return pl.pallas_call(
        _ssd_kernel,
        out_shape=(y_shape, y_shape),
        grid=(lay.batch, SSM_GROUPS // gb, nc),
        in_specs=specs(fwd) + specs(rev) + [small],
        out_specs=(pl.BlockSpec((L, gb * GROUP_WIDTH), lambda b, g, t: (b * nc + fwd(t), g)),
                   pl.BlockSpec((L, gb * GROUP_WIDTH), lambda b, g, t: (b * nc + rev(t), g))),
        scratch_shapes=[state, state],
        compiler_params=_params(("parallel", "parallel", "arbitrary"), 8 * L * gb * GROUP_WIDTH * 4),
        name="ssd_scan",
    )(xbc, xbc, xbc, dt, xbc, xbc, xbc, dt, a_log_g)


def _ssd_out_kernel(yf_ref, yr_ref, x_ref, z_ref, d_ref, g_ref, o_ref):
    y = d_ref[...] * x_ref[...] + yf_ref[...] + yr_ref[...]
    yz = y * _silu(z_ref[...])
    for g in range(SSM_GROUPS):
        cols = slice(g * GROUP_WIDTH, (g + 1) * GROUP_WIDTH)
        v = yz[:, cols]
        inv = lax.rsqrt(jnp.mean(v * v, axis=-1, keepdims=True) + RMS_EPS)
        o_ref[:, cols] = (v * inv * g_ref[:, cols]).astype(o_ref.dtype)


def _ssd_out(lay, y_f, y_r, xbc, proj, d_exp, norm_g):
    src = lay.full_row_spec()
    vec = pl.BlockSpec((1, D_MODEL), lambda i: (0, 0))
    return pl.pallas_call(
        _ssd_out_kernel,
        out_shape=jax.ShapeDtypeStruct((lay.m, D_MODEL), BF16),
        grid=(lay.n_tiles,),
        in_specs=[src, src, src, src, vec, vec],
        out_specs=pl.BlockSpec((TILE, D_MODEL), lambda i: (i, 0)),
        compiler_params=_params(("parallel",), 6 * TILE * D_MODEL * 4),
        name="ssd_gated_norm",
    )(y_f, y_r, xbc, proj, d_exp, norm_g)


def _rope_tables(seq):
    half = ATT_HEAD_DIM // 2
    nf = half // 2
    inv = ROPE_BASE ** (-(jnp.arange(nf, dtype=F32) / nf))
    rows = seq // GRID_W
    row = jnp.repeat(jnp.arange(rows, dtype=F32), GRID_W)
    col = jnp.tile(jnp.arange(GRID_W, dtype=F32), rows)
    lane = jnp.arange(2 * ATT_HEAD_DIM)
    freq = inv[lane % nf]
    pos = jnp.where((lane % ATT_HEAD_DIM) < half, row[:, None], col[:, None])
    ang = pos * freq[None, :]
    low = (lane % half) < nf
    sin = jnp.sin(ang)
    return jnp.cos(ang), jnp.where(low, -sin, 0.0), jnp.where(low, 0.0, sin)


def _rope(t, cos, sin_lo, sin_hi):
    nf = ATT_HEAD_DIM // 4
    return t * cos + pltpu.roll(t, LANES - nf, axis=1) * sin_lo + pltpu.roll(t, nf, axis=1) * sin_hi


def _dot_nt(a, b):
    return lax.dot_general(a, b, (((1,), (1,)), ((), ())), preferred_element_type=F32)


def _attn_kernel(q_ref, k_ref, v_ref, cos_ref, slo_ref, shi_ref, lam_ref, g_ref, o_ref,
                 krot_ref, kctx_ref, vb_ref, *, ctx, ctx_tiles, lam_init):
    qi = pl.program_id(2)
    heads = [slice(hh * LANES, (hh + 1) * LANES) for hh in range(ATT_HEAD_BLOCK)]

    @pl.when(qi == 0)
    def _():
        for cols in heads:
            krot_ref[:, cols] = _rope(k_ref[ctx:, cols], cos_ref[...], slo_ref[...], shi_ref[...]).astype(BF16)
        kctx_ref[...] = k_ref[:ctx, :].astype(BF16)
        vb_ref[...] = v_ref[...].astype(BF16)

    lp = lam_ref[...]
    lam = (jnp.exp(jnp.sum(lp[0:1] * lp[1:2], axis=-1, keepdims=True))
           - jnp.exp(jnp.sum(lp[2:3] * lp[3:4], axis=-1, keepdims=True)) + lam_init)
    first_map = lax.broadcasted_iota(jnp.int32, (TILE, LANES), 1) < ATT_HEAD_DIM

    def split(q):
        return jnp.where(first_map, q, 0.0).astype(BF16), jnp.where(first_map, 0.0, q).astype(BF16)

    def softmax_pv(queries, keys, values):
        scores = [_dot_nt(q, k) for q, k in zip(queries, keys)]
        mx = functools.reduce(jnp.maximum, [jnp.max(s, axis=-1, keepdims=True) for s in scores])
        e = [jnp.exp2(s - mx) for s in scores]
        total = sum(jnp.sum(x, axis=-1, keepdims=True) for x in e)
        pv = sum(jnp.dot(x.astype(BF16), v, preferred_element_type=F32) for x, v in zip(e, values))
        return pv * (1.0 / total)

    def attend(cols, queries, keys, values):
        o = softmax_pv(queries[0], keys, values) - lam * softmax_pv(queries[1], keys, values)
        inv = lax.rsqrt(jnp.mean(o * o, axis=-1, keepdims=True) + RMS_EPS)
        o_ref[:, cols] = (o * inv * g_ref[...] * (1.0 - lam_init)).astype(o_ref.dtype)

    @pl.when(qi < ctx_tiles)
    def _():
        for cols in heads:
            qn = split(q_ref[:, cols] * ATT_Q_SCALE)
            attend(cols, [[qn[0]], [qn[1]]], [kctx_ref[:, cols]], [vb_ref[:ctx, cols]])

    @pl.when(qi >= ctx_tiles)
    def _():
        start = pl.multiple_of((qi - ctx_tiles) * TILE, TILE)
        rows = pl.ds(start, TILE)
        for cols in heads:
            q = q_ref[:, cols] * ATT_Q_SCALE
            qn = split(q)
            qr = split(_rope(q, cos_ref[rows, :], slo_ref[rows, :], shi_ref[rows, :]))
            attend(cols, [[qn[0], qr[0]], [qn[1], qr[1]]], [kctx_ref[:, cols], krot_ref[:, cols]],
                   [vb_ref[:ctx, cols], vb_ref[ctx:, cols]])


def _attention(lay, qkv, tables, lam_p, subln_g, lam_init):
    width = ATT_HEAD_BLOCK * LANES
    hb = D_MODEL // width
    q0 = PROJ_Q // width
    tab = pl.BlockSpec((lay.seq, LANES), lambda b, h, i: (0, 0))
    kv_bytes = lay.nt * width * 4
    return pl.pallas_call(
        functools.partial(_attn_kernel, ctx=lay.ctx, ctx_tiles=lay.ctx_tiles, lam_init=lam_init),
        out_shape=jax.ShapeDtypeStruct((lay.m, D_MODEL), BF16),
        grid=(lay.batch, ATT_HEADS // ATT_HEAD_BLOCK, lay.tiles_per_batch),
        in_specs=[pl.BlockSpec((TILE, width), lambda b, h, i: (b * lay.tiles_per_batch + i, q0 + h)),
                  pl.BlockSpec((lay.nt, width), lambda b, h, i: (b, q0 + hb + h)),
                  pl.BlockSpec((lay.nt, width), lambda b, h, i: (b, q0 + 2 * hb + h)),
                  tab, tab, tab,
                  pl.BlockSpec((8, ATT_HEAD_DIM), lambda b, h, i: (0, 0)),
                  pl.BlockSpec((1, LANES), lambda b, h, i: (0, 0))],
        out_specs=pl.BlockSpec((TILE, width), lambda b, h, i: (b * lay.tiles_per_batch + i, h)),
        scratch_shapes=[pltpu.VMEM((lay.seq, width), BF16), pltpu.VMEM((lay.ctx, width), BF16),
                        pltpu.VMEM((lay.nt, width), BF16)],
        compiler_params=_params(("parallel", "parallel", "arbitrary"),
                                2 * kv_bytes + 3 * lay.seq * LANES * 4 + 4 * TILE * lay.nt * 4),
        name="diff_attention",
    )(qkv, qkv, qkv, *tables, lam_p, subln_g)


def _route(logits):
    lane_i = lax.broadcasted_iota(jnp.int32, logits.shape, 1)
    lane = lane_i.astype(F32)
    none = float(LANES)
    neg = -jnp.inf

    def softmax(mask):
        v = jnp.where(mask, logits, neg)
        e = jnp.exp(v - jnp.max(v, axis=-1, keepdims=True))
        return e / jnp.sum(e, axis=-1, keepdims=True)

    def top(p, mask):
        best = jnp.max(jnp.where(mask, p, -1.0), axis=-1, keepdims=True)
        idx = jnp.min(jnp.where(mask & (p == best), lane, none), axis=-1, keepdims=True)
        return best, idx

    is_group = lane_i < MOE_GROUPS
    p_grp, grp = top(softmax(is_group), is_group)
    first = MOE_GROUPS + grp * EXPERTS_PER_GROUP
    in_grp = (lane >= first) & (lane < first + EXPERTS_PER_GROUP)
    pe = softmax(in_grp)
    p1, i1 = top(pe, in_grp)
    rest = in_grp & (lane != i1)
    p2, i2 = top(pe, rest)
    denom = p1 + p2
    w1 = p_grp * p1 / denom
    w2 = p_grp * p2 / denom
    ids = jnp.where(lane_i == 0, i1 - MOE_GROUPS, jnp.where(lane_i == 1, i2 - MOE_GROUPS, 0.0))
    return ids, jnp.where(lane_i == 0, w1, jnp.where(lane_i == 1, w2, 0.0))


def _split_bf16(x):
    hi = x.astype(BF16)
    return hi, (x - hi.astype(F32)).astype(BF16)


def _mixer_out_kernel(ys_ref, ya_ref, ws_ref, wa_ref, wo_ref, gs_ref, ga_ref, h_ref, gate_ref, lg_ref, lb_ref,
                      sh_ref, sc_ref, rh_ref, rl_ref, rb_ref, h_out, u_out, ids_out, wts_out, *, alpha):
    ps = jnp.dot(ys_ref[...], ws_ref[...], preferred_element_type=F32)
    pa = jnp.dot(ya_ref[...], wa_ref[...], preferred_element_type=F32)
    m = (jax.nn.sigmoid(gs_ref[...]) * ps + jax.nn.sigmoid(ga_ref[...]) * pa).astype(BF16)
    o = jnp.dot(m, wo_ref[...], preferred_element_type=F32)
    h1 = _ln(alpha * h_ref[...] + gate_ref[...] * o) * lg_ref[...] + lb_ref[...]
    h_out[...] = h1
    u = _ln(h1) * (1.0 + sc_ref[...]) + sh_ref[...]
    u_hi, u_lo = _split_bf16(u)
    u_out[...] = u_hi.reshape(u_out.shape)
    logits = (jnp.dot(u_hi, rh_ref[...], preferred_element_type=F32)
              + (jnp.dot(u_hi, rl_ref[...], preferred_element_type=F32)
                 + jnp.dot(u_lo, rh_ref[...], preferred_element_type=F32))) + rb_ref[...]
    ids, wts = _route(logits)
    ids_out[...] = ids.astype(jnp.int32)
    wts_out[...] = wts


def _mixer_out(lay, layer, y_s, y_a, w_s, w_a, w_o, proj, h, mod3, ln_g, ln_b, r_hi, r_lo, r_b, alpha):
    row = pl.BlockSpec((TILE, D_MODEL), lambda i: (i, 0))
    vec = pl.BlockSpec((1, D_MODEL), lambda i: (0, 0))
    weight = pl.BlockSpec((None, D_MODEL, D_MODEL), lambda i: (layer, 0, 0), pipeline_mode=pl.Buffered(1))
    rw = pl.BlockSpec((D_MODEL, LANES), lambda i: (0, 0))
    small = pl.BlockSpec((TILE, LANES), lambda i: (i, 0))
    g0 = PROJ_GATES // D_MODEL
    return pl.pallas_call(
        functools.partial(_mixer_out_kernel, alpha=alpha),
        out_shape=(jax.ShapeDtypeStruct((lay.m, D_MODEL), F32), jax.ShapeDtypeStruct((lay.m, ROW_TILES, LANES), BF16),
                   jax.ShapeDtypeStruct((lay.m, LANES), jnp.int32), jax.ShapeDtypeStruct((lay.m, LANES), F32)),
        grid=(lay.n_tiles,),
        in_specs=[row, lay.full_row_spec(), weight, weight, weight,
                  lay.full_row_spec(g0), lay.full_row_spec(g0 + 1),
                  lay.full_row_spec(), lay.mod_spec(2), vec, vec, lay.mod_spec(3), lay.mod_spec(4),
                  rw, rw, pl.BlockSpec((1, LANES), lambda i: (0, 0))],
        out_specs=(row, pl.BlockSpec((TILE, ROW_TILES, LANES), lambda i: (i, 0, 0)), small, small),
        compiler_params=pltpu.CompilerParams(
            dimension_semantics=("parallel",),
            vmem_limit_bytes=int(min(VMEM_CAP, 3 * D_MODEL * D_MODEL * 2 + 2 * 7 * TILE * D_MODEL * 4 + VMEM_SLACK))),
        name="mixer_out",
    )(y_s, y_a, w_s, w_a, w_o, proj, proj, h, mod3, ln_g, ln_b, mod3, mod3, r_hi, r_lo, r_b)


def _dispatch(ids, n_tok):
    e_flat = ids[:, :MOE_TOP_K].reshape(-1)
    n_assign = n_tok * MOE_TOP_K
    onehot = (e_flat[:, None] == jnp.arange(N_EXPERTS, dtype=jnp.int32)[None, :]).astype(jnp.int32)
    csum = jnp.cumsum(onehot, axis=0)
    rank = jnp.sum(csum * onehot, axis=1) - 1
    counts = csum[-1]
    padded = (counts + MOE_BLOCK - 1) // MOE_BLOCK * MOE_BLOCK
    pend = jnp.cumsum(padded)
    pstarts = pend - padded
    dest = (pstarts[e_flat] + rank).astype(jnp.int32)
    n_blocks = n_assign // MOE_BLOCK + N_EXPERTS
    tok_flat = jnp.arange(n_assign, dtype=jnp.int32) // MOE_TOP_K
    slot_tok = jnp.zeros((n_blocks * MOE_BLOCK,), jnp.int32).at[dest].set(tok_flat)
    block_start = jnp.arange(n_blocks, dtype=jnp.int32) * MOE_BLOCK
    block_exp = jnp.minimum(jnp.sum(block_start[:, None] >= pend[None, :], axis=1), N_EXPERTS - 1).astype(jnp.int32)
    n_used = (pend[-1] // MOE_BLOCK).astype(jnp.int32).reshape(1)
    return dest, slot_tok, block_exp, n_used, n_blocks


def _row_copy(src_hbm, src_row, dst, dst_row, sem):
    return pltpu.make_async_copy(src_hbm.at[pl.ds(src_row, 1)], dst.at[pl.ds(dst_row, 1)], sem)


def _gather_rows(src_hbm, idx_ref, base, dst, sem, n_rows, start, dst_row=lambda r: r):
    for r in range(n_rows):
        cp = _row_copy(src_hbm, idx_ref[base + r], dst, dst_row(r), sem)
        if start:
            cp.start()
        else:
            cp.wait()


def _expert_kernel(bexp_ref, stok_ref, nused_ref, u_hbm, wg_ref, wu_ref, wd_ref, o_ref, xbuf, sem):
    i = pl.program_id(0)
    n = nused_ref[0]
    slot = i % 2

    def gather(blk, s, start):
        _gather_rows(u_hbm, stok_ref, blk * MOE_BLOCK, xbuf.at[s], sem.at[s], MOE_BLOCK, start)

    @pl.when(i == 0)
    def _():
        gather(0, 0, True)

    @pl.when(i < n)
    def _():
        gather(i, slot, False)
        gather(i + 1, 1 - slot, True)
        x = xbuf[slot].reshape(MOE_BLOCK, D_MODEL)
        hid = _silu(jnp.dot(x, wg_ref[...], preferred_element_type=F32)) * jnp.dot(x, wu_ref[...], preferred_element_type=F32)
        y = jnp.dot(hid.astype(BF16), wd_ref[...], preferred_element_type=F32)
        o_ref[...] = y.reshape(o_ref.shape)

    @pl.when(i == n - 1)
    def _():
        gather(i + 1, 1 - slot, False)

    @pl.when(i >= n)
    def _():
        o_ref[...] = jnp.zeros_like(o_ref)


def _experts(layer, u, block_exp, slot_tok, n_used, n_blocks, w_gate, w_up, w_down):
    wspec_in = pl.BlockSpec((None, None, D_MODEL, EXPERT_FF), lambda i, be, st, nu: (layer, be[i], 0, 0))
    grid_spec = pltpu.PrefetchScalarGridSpec(
        num_scalar_prefetch=3,
        grid=(n_blocks,),
        in_specs=[pl.BlockSpec(memory_space=pl.ANY), wspec_in, wspec_in,
                  pl.BlockSpec((None, None, EXPERT_FF, D_MODEL), lambda i, be, st, nu: (layer, be[i], 0, 0))],
        out_specs=pl.BlockSpec((MOE_BLOCK, ROW_TILES, LANES), lambda i, be, st, nu: (i, 0, 0)),
        scratch_shapes=[pltpu.VMEM((2, MOE_BLOCK, ROW_TILES, LANES), BF16), pltpu.SemaphoreType.DMA((2,))],
    )
    return pl.pallas_call(
        _expert_kernel,
        out_shape=jax.ShapeDtypeStruct((n_blocks * MOE_BLOCK, ROW_TILES, LANES), F32),
        grid_spec=grid_spec,
        compiler_params=_params(("arbitrary",), 3 * D_MODEL * EXPERT_FF * 2 + 2 * MOE_BLOCK * D_MODEL * 4),
        name="moe_experts",
    )(block_exp, slot_tok, n_used, u, w_gate, w_up, w_down)


def _combine_kernel(dest_ref, y_hbm, wts_ref, h_ref, gate_ref, lg_ref, lb_ref, sh_ref, sc_ref, h_out, u_out,
                    ybuf, sem, *, alpha, n_tiles):
    i = pl.program_id(0)
    slot = i % 2
    rows = TILE * MOE_TOP_K

    def gather(tile, s, start):
        _gather_rows(y_hbm, dest_ref, tile * rows, ybuf.at[s], sem.at[s], rows, start,
                     dst_row=lambda r: (r % MOE_TOP_K) * TILE + r // MOE_TOP_K)

    @pl.when(i == 0)
    def _():
        gather(0, 0, True)

    ahead = jnp.minimum(i + 1, n_tiles - 1)
    gather(i, slot, False)
    gather(ahead, 1 - slot, True)
    wts = wts_ref[...]
    y = (wts[:, 0:1] * ybuf[slot, 0:TILE].reshape(TILE, D_MODEL)
         + wts[:, 1:2] * ybuf[slot, TILE:2 * TILE].reshape(TILE, D_MODEL))
    h2 = _ln(alpha * h_ref[...] + gate_ref[...] * y) * lg_ref[...] + lb_ref[...]
    h_out[...] = h2
    u_out[...] = (_ln(h2) * (1.0 + sc_ref[...]) + sh_ref[...]).astype(u_out.dtype)

    @pl.when(i == n_tiles - 1)
    def _():
        gather(ahead, 1 - slot, False)


def _combine(lay, dest, y_slots, wts, h, mod3, mod3_next, ln_g, ln_b, alpha):
    row = pl.BlockSpec((TILE, D_MODEL), lambda i, d: (i, 0))
    vec = pl.BlockSpec((1, D_MODEL), lambda i, d: (0, 0))
    grid_spec = pltpu.PrefetchScalarGridSpec(
        num_scalar_prefetch=1,
        grid=(lay.n_tiles,),
        in_specs=[pl.BlockSpec(memory_space=pl.ANY), pl.BlockSpec((TILE, LANES), lambda i, d: (i, 0)), row,
                  lay.mod_spec(5), vec, vec, lay.mod_spec(0), lay.mod_spec(1)],
        out_specs=(row, row),
        scratch_shapes=[pltpu.VMEM((2, MOE_TOP_K * TILE, ROW_TILES, LANES), F32), pltpu.SemaphoreType.DMA((2,))],
    )
    return pl.pallas_call(
        functools.partial(_combine_kernel, alpha=alpha, n_tiles=lay.n_tiles),
        out_shape=(jax.ShapeDtypeStruct((lay.m, D_MODEL), F32), jax.ShapeDtypeStruct((lay.m, D_MODEL), BF16)),
        grid_spec=grid_spec,
        compiler_params=_params(("arbitrary",), 5 * TILE * D_MODEL * 4),
        name="moe_combine_residual",
    )(dest, y_slots, wts, h, mod3, ln_g, ln_b, mod3_next, mod3_next)


def _group_lanes(v):
    per_group = v.reshape(2, SSM_GROUPS, HEADS_PER_GROUP).transpose(1, 0, 2).reshape(SSM_GROUPS, 2 * HEADS_PER_GROUP)
    return jnp.pad(per_group, ((0, 0), (0, LANES - 2 * HEADS_PER_GROUP))).reshape(1, SSM_GROUPS * LANES)


def kernel(x, c, ctx, c_ctx, w_ada, b_ada, w_in, conv_w, conv_b, ssm_dt_bias, ssm_a_log, ssm_d, ssm_norm_g, lam_q1, lam_k1, lam_q2, lam_k2, attn_subln_g, w_br_ssm, w_br_att, w_out, ln1_g, ln1_b, w_router_group, b_router_group, w_router_expert, b_router_expert, w_exp_gate, w_exp_up, w_exp_down, ln2_g, ln2_b):
    batch, seq, d = x.shape
    depth = w_ada.shape[0]
    assert d == D_MODEL and batch < MOD_ROWS
    lay = _Layout(batch, seq, ctx.shape[1])
    alpha = (2.0 * depth) ** 0.25

    c_all = jnp.zeros((MOD_ROWS, d), F32).at[:batch].set(c).at[batch].set(c_ctx)
    mod = _adaln(c_all, w_ada, b_ada).reshape(depth, MOD_ROWS, 1, 6 * d)
    tables = _rope_tables(seq)

    w_proj = _regroup_w_in(w_in)
    w_bs, w_ba, w_o = w_br_ssm.astype(BF16), w_br_att.astype(BF16), w_out.astype(BF16)
    w_eg, w_eu, w_ed = w_exp_gate.astype(BF16), w_exp_up.astype(BF16), w_exp_down.astype(BF16)
    w_dt = w_in[:, :, IN_DT:IN_Q].reshape(depth, d, 2, SSM_GROUPS, HEADS_PER_GROUP).transpose(0, 1, 3, 2, 4)
    w_dt = jnp.pad(w_dt.reshape(depth, d, SSM_GROUPS, 2 * HEADS_PER_GROUP),
                   ((0, 0), (0, 0), (0, 0), (0, LANES - 2 * HEADS_PER_GROUP)))
    w_dt = w_dt.reshape(depth, d, SSM_GROUPS * LANES).astype(BF16)

    h = jnp.concatenate([ctx, x], axis=1).reshape(lay.m, d)
    u = _lnmod(lay, h, mod[0], 0, 1, BF16)
    for l in range(depth):
        lam_init = 0.8 - 0.6 * math.exp(-0.3 * l)
        proj = _matmul(u, w_proj, l)
        dt = _dt_proj(u, w_dt, l, _group_lanes(ssm_dt_bias[l]))

        xbc = _conv_silu(lay, proj, conv_w[l], conv_b[l])
        y_f, y_r = _ssd(lay, xbc, dt, _group_lanes(ssm_a_log[l]))
        out_lay = _LatentLayout(lay) if l == depth - 1 else lay
        y_s = _ssd_out(out_lay, y_f, y_r, xbc, proj, jnp.repeat(ssm_d[l], SSM_HEAD_DIM).reshape(1, d),
                       ssm_norm_g[l].reshape(1, d))

        lam_p = jnp.zeros((8, ATT_HEAD_DIM), F32).at[0].set(lam_q1[l]).at[1].set(lam_k1[l]).at[2].set(lam_q2[l]).at[3].set(lam_k2[l])
        y_a = _attention(lay, proj, tables, lam_p, attn_subln_g[l].reshape(1, LANES), lam_init)

        w_r = jnp.zeros((d, LANES), F32).at[:, :MOE_GROUPS].set(w_router_group[l]).at[:, MOE_GROUPS:MOE_GROUPS + N_EXPERTS].set(w_router_expert[l])
        b_r = jnp.zeros((1, LANES), F32).at[0, :MOE_GROUPS].set(b_router_group[l]).at[0, MOE_GROUPS:MOE_GROUPS + N_EXPERTS].set(b_router_expert[l])
        r_hi, r_lo = _split_bf16(w_r)
        h, u2, ids, wts = _mixer_out(out_lay, l, y_s, y_a, w_bs, w_ba, w_o, proj, h, mod[l], ln1_g[l].reshape(1, d),
                                     ln1_b[l].reshape(1, d), r_hi, r_lo, b_r, alpha)
        dest, slot_tok, block_exp, n_used, n_blocks = _dispatch(ids, out_lay.m)
        y_slots = _experts(l, u2, block_exp, slot_tok, n_used, n_blocks, w_eg, w_eu, w_ed)
        h, u = _combine(out_lay, dest, y_slots, wts, h, mod[l], mod[min(l + 1, depth - 1)],
                        ln2_g[l].reshape(1, d), ln2_b[l].reshape(1, d), alpha)
    return h.reshape(batch, seq, d)
```
